```python
import jax, jax.numpy as jnp
from jax import lax
import numpy as np

D_MODEL = 1024
BATCH = 8
SEQ = 8192
DEPTH = 1

CHUNK = 64
HEAD_DIM = 64
N_HEADS_FOX = 8
N_HEADS_SB = 8
D_FOX = N_HEADS_FOX * HEAD_DIM
D_SB = N_HEADS_SB * HEAD_DIM
W_IN = 3 * D_FOX + N_HEADS_FOX + 3 * D_SB + 2 * D_MODEL
Q_BLOCK = 128
N_GROUPS = 4
EXPERTS_PER_GROUP = 8
N_EXPERTS = N_GROUPS * EXPERTS_PER_GROUP
TOP_K_IN_GROUP = 2
D_EXPERT = 512
DISPATCH_BLOCK = 256
DN_ALPHA = (2.0 * DEPTH) ** 0.25
DN_BETA = (8.0 * DEPTH) ** -0.25
LN_EPS = 1e-5

kernel_name = 'hybrid_fox_stickbreaking_hmoe_block'


def layer_norm(x, g, b):
    xf = x.astype(jnp.float32)
    mu = jnp.mean(xf, axis=-1, keepdims=True)
    var = jnp.mean(jnp.square(xf - mu), axis=-1, keepdims=True)
    return ((xf - mu) * lax.rsqrt(var + LN_EPS) * g.astype(jnp.float32) + b.astype(jnp.float32)).astype(x.dtype)


def to_heads(t, n_heads):
    B, S, _ = t.shape
    return t.reshape(B, S, n_heads, HEAD_DIM).transpose(0, 2, 1, 3)


def from_heads(t):
    B, H, S, Dh = t.shape
    return t.transpose(0, 2, 1, 3).reshape(B, S, H * Dh)


def sweep_query_blocks(block_fn, seq_len):
    out = lax.map(block_fn, jnp.arange(seq_len // Q_BLOCK))
    out = jnp.moveaxis(out, 0, 2)
    B, H, nb, Q, Dh = out.shape
    return out.reshape(B, H, nb * Q, Dh)


def forgetting_attention(q, k, v, log_f):
    B, H, S, Dh = q.shape
    scale = Dh ** -0.5
    F = jnp.cumsum(log_f.astype(jnp.float32), axis=-1)
    kf = k.astype(jnp.float32)
    vf = v.astype(jnp.float32)
    s_pos = jnp.arange(S)

    def block(i):
        start = i * Q_BLOCK
        qb = lax.dynamic_slice_in_dim(q, start, Q_BLOCK, axis=2).astype(jnp.float32)
        Fq = lax.dynamic_slice_in_dim(F, start, Q_BLOCK, axis=2)
        t_pos = start + jnp.arange(Q_BLOCK)
        logits = jnp.einsum('bhqd,bhkd->bhqk', qb, kf) * scale + Fq[..., :, None] - F[..., None, :]
        causal = s_pos[None, :] <= t_pos[:, None]
        p = jax.nn.softmax(jnp.where(causal, logits, -jnp.inf), axis=-1)
        return jnp.einsum('bhqk,bhkd->bhqd', p, vf)

    return sweep_query_blocks(block, S).astype(q.dtype)


def stick_breaking_attention(q, k, v):
    B, H, S, Dh = q.shape
    scale = Dh ** -0.5
    kf = k.astype(jnp.float32)
    vf = v.astype(jnp.float32)
    s_pos = jnp.arange(S)

    def block(i):
        start = i * Q_BLOCK
        qb = lax.dynamic_slice_in_dim(q, start, Q_BLOCK, axis=2).astype(jnp.float32)
        t_pos = start + jnp.arange(Q_BLOCK)
        z = jnp.einsum('bhqd,bhkd->bhqk', qb, kf) * scale
        strict = s_pos[None, :] < t_pos[:, None]
        log_one_minus_beta = jnp.where(strict, jax.nn.log_sigmoid(-z), 0.0)
        suffix = lax.cumsum(log_one_minus_beta, axis=3, reverse=True) - log_one_minus_beta
        a = jnp.where(strict, jnp.exp(jax.nn.log_sigmoid(z) + suffix), 0.0)
        return jnp.einsum('bhqk,bhkd->bhqd', a, vf)

    return sweep_query_blocks(block, S).astype(q.dtype)


def hybrid_mixer(x, w_in, b_forget, w_up_fox, w_up_sb, w_out):
    proj = jnp.einsum('bsd,de->bse', x, w_in)
    cuts = [D_FOX, 2 * D_FOX, 3 * D_FOX, 3 * D_FOX + N_HEADS_FOX,
            3 * D_FOX + N_HEADS_FOX + D_SB, 3 * D_FOX + N_HEADS_FOX + 2 * D_SB,
            3 * D_FOX + N_HEADS_FOX + 3 * D_SB, 3 * D_FOX + N_HEADS_FOX + 3 * D_SB + D_MODEL]
    q_f, k_f, v_f, f_logit, q_s, k_s, v_s, g_fox, g_sb = jnp.split(proj, cuts, axis=-1)
    log_f = jax.nn.log_sigmoid(f_logit.astype(jnp.float32) + b_forget.astype(jnp.float32))
    y_fox = forgetting_attention(to_heads(q_f, N_HEADS_FOX), to_heads(k_f, N_HEADS_FOX),
                                 to_heads(v_f, N_HEADS_FOX), jnp.transpose(log_f, (0, 2, 1)))
    y_sb = stick_breaking_attention(to_heads(q_s, N_HEADS_SB), to_heads(k_s, N_HEADS_SB),
                                    to_heads(v_s, N_HEADS_SB))
    up_fox = jnp.einsum('bse,ed->bsd', from_heads(y_fox), w_up_fox)
    up_sb = jnp.einsum('bse,ed->bsd', from_heads(y_sb), w_up_sb)
    merged = jax.nn.sigmoid(g_fox) * up_fox + jax.nn.sigmoid(g_sb) * up_sb
    return jnp.einsum('bsd,de->bse', merged, w_out)


def hierarchical_moe(h, w_group, b_group, w_expert_router, b_expert_router, w1, w3, w2):
    B, S, D = h.shape
    N = B * S
    M = N * TOP_K_IN_GROUP
    hf = h.reshape(N, D)
    g_logits = jnp.einsum('nd,dg->ng', hf, w_group).astype(jnp.float32) + b_group.astype(jnp.float32)
    g_prob = jax.nn.softmax(g_logits, axis=-1)
    g_sel = jnp.argmax(g_logits, axis=-1)
    g_gate = jnp.take_along_axis(g_prob, g_sel[:, None], axis=-1)[:, 0]
    e_all = (jnp.einsum('nd,de->ne', hf, w_expert_router).astype(jnp.float32)
             + b_expert_router.astype(jnp.float32)).reshape(N, N_GROUPS, EXPERTS_PER_GROUP)
    e_logits = jnp.take_along_axis(e_all, g_sel[:, None, None], axis=1)[:, 0]
    top_p, top_i = lax.top_k(jax.nn.softmax(e_logits, axis=-1), TOP_K_IN_GROUP)
    top_p = top_p / jnp.sum(top_p, axis=-1, keepdims=True)
    gate = (g_gate[:, None] * top_p).reshape(M)
    expert_id = (g_sel[:, None] * EXPERTS_PER_GROUP + top_i).reshape(M)
    token_id = jnp.arange(M) // TOP_K_IN_GROUP
    order = jnp.argsort(expert_id)
    s_e, s_tok, s_w = expert_id[order], token_id[order], gate[order]
    counts = jnp.bincount(expert_id, length=N_EXPERTS)
    padded = ((counts + DISPATCH_BLOCK - 1) // DISPATCH_BLOCK) * DISPATCH_BLOCK
    starts = jnp.cumsum(counts) - counts
    pstarts = jnp.cumsum(padded) - padded
    dest = pstarts[s_e] + (jnp.arange(M) - starts[s_e])
    n_blocks = (M + N_EXPERTS * (DISPATCH_BLOCK - 1) + DISPATCH_BLOCK - 1) // DISPATCH_BLOCK
    P = n_blocks * DISPATCH_BLOCK
    x_pad = jnp.zeros((P, D), h.dtype).at[dest].set(hf[s_tok])
    block_expert = jnp.repeat(jnp.arange(N_EXPERTS), padded // DISPATCH_BLOCK,
                              total_repeat_length=n_blocks)

    def expert_block(args):
        xb, e = args
        return (jax.nn.silu(xb @ w1[e]) * (xb @ w3[e])) @ w2[e]

    y_pad = lax.map(expert_block, (x_pad.reshape(n_blocks, DISPATCH_BLOCK, D), block_expert))
    y_assign = y_pad.reshape(P, D)[dest] * s_w[:, None].astype(h.dtype)
    y = jnp.zeros((N, D), h.dtype).at[s_tok].add(y_assign)
    return y.reshape(B, S, D)


def setup_inputs(seed: int = 0) -> dict:
    key = jax.random.key(seed)
    ks = jax.random.split(key, 24)
    f32 = jnp.float32
    L, D = DEPTH, D_MODEL
    std = D ** -0.5

    def nrm(k, shape, scale):
        return jax.random.normal(k, shape, f32) * scale

    x = nrm(ks[0], (BATCH, SEQ, D), 1.0)
    ln_in_g = 1.0 + nrm(ks[1], (D,), 0.02)
    ln_in_b = nrm(ks[2], (D,), 0.02)
    w_in = jnp.concatenate([
        nrm(ks[3], (L, D, 2 * D_FOX), std),
        nrm(ks[4], (L, D, D_FOX), std * DN_BETA),
        nrm(ks[5], (L, D, N_HEADS_FOX), std),
        nrm(ks[6], (L, D, 2 * D_SB), std),
        nrm(ks[7], (L, D, D_SB), std * DN_BETA),
        nrm(ks[8], (L, D, 2 * D), std),
    ], axis=-1)
    b_forget = jax.random.uniform(ks[9], (L, N_HEADS_FOX), f32, 1.0, 5.0)
    w_up_fox = nrm(ks[10], (L, D_FOX, D), D_FOX ** -0.5)
    w_up_sb = nrm(ks[11], (L, D_SB, D), D_SB ** -0.5)
    w_out = nrm(ks[12], (L, D, D), std * DN_BETA)
    ln1_g = 1.0 + nrm(ks[13], (L, D), 0.02)
    ln1_b = nrm(ks[14], (L, D), 0.02)
    w_group = nrm(ks[15], (L, D, N_GROUPS), std)
    b_group = nrm(ks[16], (L, N_GROUPS), 0.01)
    w_expert_router = nrm(ks[17], (L, D, N_EXPERTS), std)
    b_expert_router = nrm(ks[18], (L, N_EXPERTS), 0.01)
    w1 = nrm(ks[19], (L, N_EXPERTS, D, D_EXPERT), std * DN_BETA)
    w3 = nrm(ks[20], (L, N_EXPERTS, D, D_EXPERT), std * DN_BETA)
    w2 = nrm(ks[21], (L, N_EXPERTS, D_EXPERT, D), (D_EXPERT ** -0.5) * DN_BETA)
    ln2_g = 1.0 + nrm(ks[22], (L, D), 0.02)
    ln2_b = nrm(ks[23], (L, D), 0.02)
    return {'x': x, 'ln_in_g': ln_in_g, 'ln_in_b': ln_in_b, 'w_in': w_in, 'b_forget': b_forget,
            'w_up_fox': w_up_fox, 'w_up_sb': w_up_sb, 'w_out': w_out, 'ln1_g': ln1_g, 'ln1_b': ln1_b,
            'w_group': w_group, 'b_group': b_group, 'w_expert_router': w_expert_router,
            'b_expert_router': b_expert_router, 'w1': w1, 'w3': w3, 'w2': w2,
            'ln2_g': ln2_g, 'ln2_b': ln2_b}


def reference(x, ln_in_g, ln_in_b, w_in, b_forget, w_up_fox, w_up_sb, w_out, ln1_g, ln1_b,
              w_group, b_group, w_expert_router, b_expert_router, w1, w3, w2, ln2_g, ln2_b):
    h = layer_norm(x, ln_in_g, ln_in_b)
    for l in range(DEPTH):
        mix = hybrid_mixer(h, w_in[l], b_forget[l], w_up_fox[l], w_up_sb[l], w_out[l])
        h = layer_norm(DN_ALPHA * h + mix, ln1_g[l], ln1_b[l])
        ffn = hierarchical_moe(h, w_group[l], b_group[l], w_expert_router[l], b_expert_router[l],
                               w1[l], w3[l], w2[l])
        h = layer_norm(DN_ALPHA * h + ffn, ln2_g[l], ln2_b[l])
    return h
```

```python
import functools

import jax
import jax.numpy as jnp
from jax import lax
from jax.experimental import pallas as pl
from jax.experimental.pallas import tpu as pltpu

HEAD_DIM = 64
N_HEADS = 8
D_BRANCH = N_HEADS * HEAD_DIM
N_GROUPS = 4
EXPERTS_PER_GROUP = 8
N_EXPERTS = N_GROUPS * EXPERTS_PER_GROUP
TOP_K = 2
DISPATCH_BLOCK = 256
DEPTH = 1
DN_ALPHA = (2.0 * DEPTH) ** 0.25
LN_EPS = 1e-5

LANES = 128
TOKEN_TILE = 512
ATTN_TILE = 256
COMBINE_TILE = 256
VMEM_LIMIT = 56 * 1024 * 1024
NEG_BIG = -1e30

_f32 = jnp.float32
_bf16 = jnp.bfloat16


def _layer_norm(x, g, b):
    mu = jnp.mean(x, axis=-1, keepdims=True)
    xc = x - mu
    var = jnp.mean(xc * xc, axis=-1, keepdims=True)
    return xc * lax.rsqrt(var + LN_EPS) * g + b


def _split2(x):
    hi = x.astype(_bf16)
    lo = (x - hi.astype(_f32)).astype(_bf16)
    return hi, lo


def _split3(x):
    hi = x.astype(_bf16)
    r = x - hi.astype(_f32)
    mid = r.astype(_bf16)
    lo = (r - mid.astype(_f32)).astype(_bf16)
    return hi, mid, lo


def _dot(a, b):
    return jnp.dot(a, b, preferred_element_type=_f32)


def _dot_nt(a, b):
    return lax.dot_general(a, b, (((1,), (1,)), ((), ())), preferred_element_type=_f32)


def _inproj_kernel(x_ref, g_ref, b_ref, w_ref, wf_hi_ref, wf_lo_ref, bf_ref, tri_ref,
                   qf_ref, kf_ref, vf_ref, qs_ref, ks_ref, vs_ref, gf_ref, gs_ref, fcum_ref,
                   carry_ref, *, tiles_per_seq):
    i = pl.program_id(0)
    h = _layer_norm(x_ref[...], g_ref[...], b_ref[...])
    h_hi, h_lo = _split2(h)

    col = 0
    for ref in (qf_ref, kf_ref, vf_ref, qs_ref, ks_ref, vs_ref, gf_ref, gs_ref):
        width = ref.shape[-1]
        ref[...] = _dot(h_hi, w_ref[:, col:col + width]).astype(ref.dtype)
        col += width

    f_logit = (_dot(h_hi, wf_hi_ref[...]) + _dot(h_lo, wf_hi_ref[...])
               + _dot(h_hi, wf_lo_ref[...])) + bf_ref[...]
    log_f = jnp.minimum(f_logit, 0.0) - jnp.log(1.0 + jnp.exp(-jnp.abs(f_logit)))

    p0, p1, p2 = _split3(log_f)
    tri = tri_ref[...]
    cum = _dot(tri, p0) + _dot(tri, p1) + _dot(tri, p2)

    @pl.when(i % tiles_per_seq == 0)
    def _():
        carry_ref[...] = jnp.zeros_like(carry_ref)

    cum = cum + carry_ref[...]
    fcum_ref[...] = cum
    carry_ref[...] = cum[-1:, :]


def _inproj(x2, ln_g, ln_b, w_main, wf_hi, wf_lo, bf_pad, seq_len):
    n, d = x2.shape
    tm = TOKEN_TILE
    assert n % tm == 0 and seq_len % tm == 0
    tri = (lax.broadcasted_iota(jnp.int32, (tm, tm), 1)
           <= lax.broadcasted_iota(jnp.int32, (tm, tm), 0)).astype(_bf16)
    const = lambda i: (0, 0)
    row = lambda i: (i, 0)
    widths = (D_BRANCH,) * 6 + (d, d)
    out_shape = [jax.ShapeDtypeStruct((n, w), _bf16) for w in widths]
    out_shape.append(jax.ShapeDtypeStruct((n, LANES), _f32))
    out_specs = [pl.BlockSpec((tm, w), row) for w in widths] + [pl.BlockSpec((tm, LANES), row)]
    return pl.pallas_call(
        functools.partial(_inproj_kernel, tiles_per_seq=seq_len // tm),
        grid=(n // tm,),
        in_specs=[
            pl.BlockSpec((tm, d), row),
            pl.BlockSpec((1, d), const),
            pl.BlockSpec((1, d), const),
            pl.BlockSpec(w_main.shape, const),
            pl.BlockSpec(wf_hi.shape, const),
            pl.BlockSpec(wf_lo.shape, const),
            pl.BlockSpec((1, LANES), const),
            pl.BlockSpec((tm, tm), const),
        ],
        out_specs=out_specs,
        out_shape=out_shape,
        scratch_shapes=[pltpu.VMEM((1, LANES), _f32)],
        compiler_params=pltpu.CompilerParams(
            dimension_semantics=("arbitrary",), vmem_limit_bytes=VMEM_LIMIT),
        name="inproj",
    )(x2, ln_g, ln_b, w_main, wf_hi, wf_lo, bf_pad, tri)


def _head_masks(shape):
    lane = lax.broadcasted_iota(jnp.int32, shape, 1)
    return lane < HEAD_DIM


def _fox_kernel(q_ref, k_ref, v_ref, frow_ref, o_ref, m_ref, l_ref, acc_ref, *, tile):
    i = pl.program_id(2)
    q = q_ref[0]
    first_head = _head_masks(q.shape)
    row = lax.broadcasted_iota(jnp.int32, (tile, tile), 0)
    colm = lax.broadcasted_iota(jnp.int32, (tile, tile), 1)
    causal = colm <= row

    for h in range(2):
        qh = jnp.where(first_head if h == 0 else jnp.logical_not(first_head), q, jnp.zeros_like(q))
        m_ref[...] = jnp.full(m_ref.shape, NEG_BIG, _f32)
        l_ref[...] = jnp.zeros(l_ref.shape, _f32)
        acc_ref[...] = jnp.zeros(acc_ref.shape, _f32)

        def step(j, masked):
            start = pl.multiple_of(j * tile, tile)
            kt = k_ref[0, pl.ds(start, tile), :]
            vt = v_ref[0, pl.ds(start, tile), :]
            t = _dot_nt(qh, kt) - frow_ref[0, 0, j, h:h + 1, :]
            if masked:
                t = jnp.where(causal, t, NEG_BIG)
            m_old = m_ref[...]
            m_new = jnp.maximum(m_old, jnp.max(t, axis=-1, keepdims=True))
            alpha = jnp.exp(m_old - m_new)
            p = jnp.exp(t - m_new)
            l_ref[...] = alpha * l_ref[...] + jnp.sum(p, axis=-1, keepdims=True)
            acc_ref[...] = alpha * acc_ref[...] + _dot(p.astype(_bf16), vt)
            m_ref[...] = m_new

        def body(j, carry):
            step(j, False)
            return carry

        lax.fori_loop(0, i, body, 0)
        step(i, True)
        res = acc_ref[...] / l_ref[...]
        if h == 0:
            res0 = res
        else:
            o_ref[0] = jnp.where(first_head, res0, res).astype(o_ref.dtype)


def _fox_attn(q, k, v, frow, tile):
    b, s, _ = q.shape
    n_pairs = D_BRANCH // LANES
    nk = s // tile
    return pl.pallas_call(
        functools.partial(_fox_kernel, tile=tile),
        grid=(b, n_pairs, nk),
        in_specs=[
            pl.BlockSpec((1, tile, LANES), lambda bi, hp, i: (bi, i, hp)),
            pl.BlockSpec((1, s, LANES), lambda bi, hp, i: (bi, 0, hp)),
            pl.BlockSpec((1, s, LANES), lambda bi, hp, i: (bi, 0, hp)),
            pl.BlockSpec((1, 1, nk, 2, tile), lambda bi, hp, i: (bi, hp, 0, 0, 0)),
        ],
        out_specs=pl.BlockSpec((1, tile, LANES), lambda bi, hp, i: (bi, i, hp)),
        out_shape=jax.ShapeDtypeStruct((b, s, D_BRANCH), _bf16),
        scratch_shapes=[pltpu.VMEM((tile, 1), _f32), pltpu.VMEM((tile, 1), _f32),
                        pltpu.VMEM((tile, LANES), _f32)],
        compiler_params=pltpu.CompilerParams(
            dimension_semantics=("arbitrary", "arbitrary", "arbitrary"),
            vmem_limit_bytes=VMEM_LIMIT),
        name="fox_attn",
    )(q, k, v, frow)


def _sb_kernel(q_ref, k_ref, v_ref, tri_ref, o_ref, c_ref, acc_ref, *, tile):
    i = pl.program_id(2)
    q = q_ref[0]
    first_head = _head_masks(q.shape)
    row = lax.broadcasted_iota(jnp.int32, (tile, tile), 0)
    colm = lax.broadcasted_iota(jnp.int32, (tile, tile), 1)
    strict = colm < row
    tri = tri_ref[...]

    for h in range(2):
        qh = jnp.where(first_head if h == 0 else jnp.logical_not(first_head), q, jnp.zeros_like(q))
        c_ref[...] = jnp.zeros(c_ref.shape, _f32)
        acc_ref[...] = jnp.zeros(acc_ref.shape, _f32)

        def step(j, masked):
            start = pl.multiple_of(j * tile, tile)
            kt = k_ref[0, pl.ds(start, tile), :]
            vt = v_ref[0, pl.ds(start, tile), :]
            z = _dot_nt(qh, kt)
            sp = jnp.maximum(z, 0.0) + jnp.log(1.0 + jnp.exp(-jnp.abs(z)))
            log_beta = z - sp
            if masked:
                sp = jnp.where(strict, sp, 0.0)
            sp_hi, sp_lo = _split2(sp)
            later = _dot(sp_hi, tri) + _dot(sp_lo, tri)
            c = c_ref[...]
            a = jnp.exp(log_beta - later - c)
            if masked:
                a = jnp.where(strict, a, 0.0)
            acc_ref[...] += _dot(a.astype(_bf16), vt)
            c_ref[...] = c + later[:, 0:1] + sp[:, 0:1]

        step(i, True)

        def body(jj, carry):
            step(i - 1 - jj, False)
            return carry

        lax.fori_loop(0, i, body, 0)
        if h == 0:
            res0 = acc_ref[...]
        else:
            o_ref[0] = jnp.where(first_head, res0, acc_ref[...]).astype(o_ref.dtype)


def _sb_attn(q, k, v, tile):
    b, s, _ = q.shape
    n_pairs = D_BRANCH // LANES
    nk = s // tile
    tri = (lax.broadcasted_iota(jnp.int32, (tile, tile), 0)
           > lax.broadcasted_iota(jnp.int32, (tile, tile), 1)).astype(_bf16)
    return pl.pallas_call(
        functools.partial(_sb_kernel, tile=tile),
        grid=(b, n_pairs, nk),
        in_specs=[
            pl.BlockSpec((1, tile, LANES), lambda bi, hp, i: (bi, i, hp)),
            pl.BlockSpec((1, s, LANES), lambda bi, hp, i: (bi, 0, hp)),
            pl.BlockSpec((1, s, LANES), lambda bi, hp, i: (bi, 0, hp)),
            pl.BlockSpec((tile, tile), lambda bi, hp, i: (0, 0)),
        ],
        out_specs=pl.BlockSpec((1, tile, LANES), lambda bi, hp, i: (bi, i, hp)),
        out_shape=jax.ShapeDtypeStruct((b, s, D_BRANCH), _bf16),
        scratch_shapes=[pltpu.VMEM((tile, 1), _f32), pltpu.VMEM((tile, LANES), _f32)],
        compiler_params=pltpu.CompilerParams(
            dimension_semantics=("arbitrary", "arbitrary", "arbitrary"),
            vmem_limit_bytes=VMEM_LIMIT),
        name="sb_attn",
    )(q, k, v, tri)


def _post_attn_kernel(x_ref, lng_ref, lnb_ref, yf_ref, ys_ref, gf_ref, gs_ref,
                      wuf_ref, wus_ref, wo_ref, g1_ref, b1_ref, wr_hi_ref, wr_lo_ref, br_ref,
                      h1_ref, route_ref):
    h0 = _layer_norm(x_ref[...], lng_ref[...], lnb_ref[...])
    up_f = _dot(yf_ref[...], wuf_ref[...])
    up_s = _dot(ys_ref[...], wus_ref[...])
    sig_f = 1.0 / (1.0 + jnp.exp(-gf_ref[...].astype(_f32)))
    sig_s = 1.0 / (1.0 + jnp.exp(-gs_ref[...].astype(_f32)))
    merged = sig_f * up_f + sig_s * up_s
    mix = _dot(merged.astype(_bf16), wo_ref[...])
    h1 = _layer_norm(DN_ALPHA * h0 + mix, g1_ref[...], b1_ref[...])
    h1_ref[...] = h1

    h_hi, h_lo = _split2(h1)
    logits = (_dot(h_hi, wr_hi_ref[...]) + _dot(h_lo, wr_hi_ref[...])
              + _dot(h_hi, wr_lo_ref[...])) + br_ref[...]
    lane = lax.broadcasted_iota(jnp.int32, logits.shape, 1)
    lane_f = lane.astype(_f32)
    big = float(LANES)

    g_log = jnp.where(lane < N_GROUPS, logits, -jnp.inf)
    g_max = jnp.max(g_log, axis=-1, keepdims=True)
    g_sel = jnp.min(jnp.where(g_log == g_max, lane_f, big), axis=-1, keepdims=True)
    g_gate = 1.0 / jnp.sum(jnp.exp(g_log - g_max), axis=-1, keepdims=True)

    lo_lane = N_GROUPS + g_sel * EXPERTS_PER_GROUP
    in_grp = jnp.logical_and(lane_f >= lo_lane, lane_f < lo_lane + EXPERTS_PER_GROUP)
    e_log = jnp.where(in_grp, logits, -jnp.inf)
    e1 = jnp.max(e_log, axis=-1, keepdims=True)
    i1 = jnp.min(jnp.where(e_log == e1, lane_f, big), axis=-1, keepdims=True)
    e_log2 = jnp.where(lane_f == i1, -jnp.inf, e_log)
    e2 = jnp.max(e_log2, axis=-1, keepdims=True)
    i2 = jnp.min(jnp.where(e_log2 == e2, lane_f, big), axis=-1, keepdims=True)
    d = jnp.exp(e2 - e1)
    w1 = 1.0 / (1.0 + d)
    w2 = d * w1

    route = jnp.where(lane == 0, g_gate * w1, 0.0)
    route = jnp.where(lane == 1, g_gate * w2, route)
    route = jnp.where(lane == 2, i1 - N_GROUPS, route)
    route = jnp.where(lane == 3, i2 - N_GROUPS, route)
    route_ref[...] = route


def _post_attn(x2, lng, lnb, yf, ys, gf, gs, wuf, wus, wo, g1, b1, wr_hi, wr_lo, br):
    n, d = x2.shape
    tm = TOKEN_TILE
    const = lambda i: (0, 0)
    row = lambda i: (i, 0)
    vec = pl.BlockSpec((1, d), const)
    return pl.pallas_call(
        _post_attn_kernel,
        grid=(n // tm,),
        in_specs=[
            pl.BlockSpec((tm, d), row), vec, vec,
            pl.BlockSpec((tm, D_BRANCH), row), pl.BlockSpec((tm, D_BRANCH), row),
            pl.BlockSpec((tm, d), row), pl.BlockSpec((tm, d), row),
            pl.BlockSpec(wuf.shape, const), pl.BlockSpec(wus.shape, const),
            pl.BlockSpec(wo.shape, const), vec, vec,
            pl.BlockSpec(wr_hi.shape, const), pl.BlockSpec(wr_lo.shape, const),
            pl.BlockSpec((1, LANES), const),
        ],
        out_specs=[pl.BlockSpec((tm, d), row), pl.BlockSpec((tm, LANES), row)],
        out_shape=[jax.ShapeDtypeStruct((n, d), _f32), jax.ShapeDtypeStruct((n, LANES), _f32)],
        compiler_params=pltpu.CompilerParams(
            dimension_semantics=("arbitrary",), vmem_limit_bytes=VMEM_LIMIT),
        name="post_attn",
    )(x2, lng, lnb, yf, ys, gf, gs, wuf, wus, wo, g1, b1, wr_hi, wr_lo, br)


def _experts_kernel(blk_expert_ref, n_used_ref, src_ref, src_next_ref, h_hbm, w1_ref, w3_ref, w2_ref,
                    y_ref, buf_ref, sem_ref):
    del blk_expert_ref
    b = pl.program_id(0)
    nb = pl.num_programs(0)
    n_used = n_used_ref[0]
    rows = buf_ref.shape[1]

    def row_copy(idx_ref, r, slot):
        return pltpu.make_async_copy(h_hbm.at[pl.ds(idx_ref[0, 0, r], 1), :],
                                     buf_ref.at[slot, pl.ds(r, 1), :], sem_ref.at[slot])

    def start_block(idx_ref, slot):
        def body(r, carry):
            row_copy(idx_ref, r, slot).start()
            return carry
        lax.fori_loop(0, rows, body, 0)

    def wait_block(idx_ref, slot):
        def body(r, carry):
            row_copy(idx_ref, r, slot).wait()
            return carry
        lax.fori_loop(0, rows, body, 0)

    slot = b % 2

    @pl.when(jnp.logical_and(b == 0, n_used > 0))
    def _():
        start_block(src_ref, 0)

    @pl.when(jnp.logical_and(b + 1 < nb, b + 1 < n_used))
    def _():
        start_block(src_next_ref, 1 - slot)

    @pl.when(b < n_used)
    def _():
        wait_block(src_ref, slot)
        xb = buf_ref[slot].astype(_bf16)
        a = _dot(xb, w1_ref[0])
        g = _dot(xb, w3_ref[0])
        hidden = (a * (1.0 / (1.0 + jnp.exp(-a)))) * g
        y_ref[...] = _dot(hidden.astype(_bf16), w2_ref[0])

    @pl.when(b >= n_used)
    def _():
        y_ref[...] = jnp.zeros_like(y_ref)


def _experts(blk_expert, n_used, src3, h1, w1b, w3b, w2b):
    n_blocks = blk_expert.shape[0]
    rows = DISPATCH_BLOCK
    d = h1.shape[1]
    de = w1b.shape[2]
    grid_spec = pltpu.PrefetchScalarGridSpec(
        num_scalar_prefetch=2,
        grid=(n_blocks,),
        in_specs=[
            pl.BlockSpec((1, 1, rows), lambda b, be, nu: (b, 0, 0), memory_space=pltpu.SMEM),
            pl.BlockSpec((1, 1, rows), lambda b, be, nu: (jnp.minimum(b + 1, n_blocks - 1), 0, 0),
                         memory_space=pltpu.SMEM),
            pl.BlockSpec(memory_space=pl.ANY),
            pl.BlockSpec((1, d, de), lambda b, be, nu: (be[b], 0, 0)),
            pl.BlockSpec((1, d, de), lambda b, be, nu: (be[b], 0, 0)),
            pl.BlockSpec((1, de, d), lambda b, be, nu: (be[b], 0, 0)),
        ],
        out_specs=pl.BlockSpec((rows, d), lambda b, be, nu: (b, 0)),
        scratch_shapes=[pltpu.VMEM((2, rows, d), _f32), pltpu.SemaphoreType.DMA((2,))],
    )
    return pl.pallas_call(
        _experts_kernel,
        grid_spec=grid_spec,
        out_shape=jax.ShapeDtypeStruct((n_blocks * rows, d), _f32),
        compiler_params=pltpu.CompilerParams(
            dimension_semantics=("arbitrary",), vmem_limit_bytes=VMEM_LIMIT),
        name="experts",
    )(blk_expert, n_used, src3, src3, h1, w1b, w3b, w2b)


def _combine_kernel(pos_ref, h1_ref, gate_ref, g2_ref, b2_ref, y_hbm, o_ref, buf_ref, sem_ref):
    tm = o_ref.shape[0]

    def row_copy(r):
        return pltpu.make_async_copy(y_hbm.at[pl.ds(pos_ref[0, 0, r], 1), :],
                                     buf_ref.at[pl.ds(r, 1), :], sem_ref.at[0])

    def start(r, carry):
        row_copy(r).start()
        return carry

    def wait(r, carry):
        row_copy(r).wait()
        return carry

    lax.fori_loop(0, TOP_K * tm, start, 0)
    lax.fori_loop(0, TOP_K * tm, wait, 0)
    gate = gate_ref[...]
    ffn = gate[:, 0:1] * buf_ref[0:tm, :] + gate[:, 1:2] * buf_ref[tm:2 * tm, :]
    o_ref[...] = _layer_norm(DN_ALPHA * h1_ref[...] + ffn, g2_ref[...], b2_ref[...])


def _combine(pos3, h1, route, g2, b2, y_pad):
    n, d = h1.shape
    tm = COMBINE_TILE
    const = lambda i: (0, 0)
    row = lambda i: (i, 0)
    return pl.pallas_call(
        _combine_kernel,
        grid=(n // tm,),
        in_specs=[
            pl.BlockSpec((1, 1, TOP_K * tm), lambda i: (i, 0, 0), memory_space=pltpu.SMEM),
            pl.BlockSpec((tm, d), row),
            pl.BlockSpec((tm, LANES), row),
            pl.BlockSpec((1, d), const), pl.BlockSpec((1, d), const),
            pl.BlockSpec(memory_space=pl.ANY),
        ],
        out_specs=pl.BlockSpec((tm, d), row),
        out_shape=jax.ShapeDtypeStruct((n, d), _f32),
        scratch_shapes=[pltpu.VMEM((TOP_K * tm, d), _f32), pltpu.SemaphoreType.DMA((1,))],
        compiler_params=pltpu.CompilerParams(
            dimension_semantics=("arbitrary",), vmem_limit_bytes=VMEM_LIMIT),
        name="combine",
    )(pos3, h1, route, g2, b2, y_pad)


def _pad_lanes(w):
    return jnp.pad(w, ((0, 0), (0, LANES - w.shape[1])))


def _dispatch_plan(expert_id):
    n = expert_id.shape[0]
    m = n * TOP_K
    flat = expert_id.reshape(m)
    onehot = (flat[:, None] == jnp.arange(N_EXPERTS, dtype=jnp.int32)[None, :]).astype(jnp.int32)
    incl = jnp.cumsum(onehot, axis=0)
    counts = incl[-1]
    rank = jnp.sum((incl - onehot) * onehot, axis=1)
    blocks_per = (counts + DISPATCH_BLOCK - 1) // DISPATCH_BLOCK
    blk_end = jnp.cumsum(blocks_per)
    pstarts = (blk_end - blocks_per) * DISPATCH_BLOCK
    dest = pstarts[flat] + rank
    n_blocks = (m + N_EXPERTS * (DISPATCH_BLOCK - 1) + DISPATCH_BLOCK - 1) // DISPATCH_BLOCK
    n_used = blk_end[-1]
    src = jnp.zeros((n_blocks * DISPATCH_BLOCK,), jnp.int32).at[dest].set(
        jnp.arange(m, dtype=jnp.int32) // TOP_K)
    blk = jnp.arange(n_blocks, dtype=jnp.int32)
    blk_expert = jnp.sum((blk[:, None] >= blk_end[None, :]).astype(jnp.int32), axis=1)
    last_expert = jnp.sum((jnp.maximum(n_used - 1, 0) >= blk_end).astype(jnp.int32))
    blk_expert = jnp.where(blk < n_used, blk_expert, last_expert).astype(jnp.int32)
    return dest.reshape(n, TOP_K), src.reshape(n_blocks, 1, DISPATCH_BLOCK), blk_expert, \
        n_used.astype(jnp.int32).reshape(1)


def kernel(x, ln_in_g, ln_in_b, w_in, b_forget, w_up_fox, w_up_sb, w_out, ln1_g, ln1_b, w_group, b_group,
           w_expert_router, b_expert_router, w1, w3, w2, ln2_g, ln2_b):
    bsz, seq, d = x.shape
    n = bsz * seq
    assert w_in.shape[0] == DEPTH
    x2 = x.reshape(n, d)
    tile = ATTN_TILE
    scale = HEAD_DIM ** -0.5

    wi = w_in[0]
    c = D_BRANCH
    off_f = 3 * c
    off_s = off_f + N_HEADS
    off_g = off_s + 3 * c
    w_main = jnp.concatenate([
        wi[:, 0:c] * scale, wi[:, c:3 * c],
        wi[:, off_s:off_s + c] * scale, wi[:, off_s + c:off_g],
        wi[:, off_g:],
    ], axis=1).astype(_bf16)
    wf = _pad_lanes(wi[:, off_f:off_s])
    wf_hi, wf_lo = _split2(wf)
    bf_pad = _pad_lanes(b_forget[0][None, :])

    row = lambda v: v.reshape(1, -1)
    qf, kf, vf, qs, ks, vs, gf, gs, fcum = _inproj(
        x2, row(ln_in_g), row(ln_in_b), w_main, wf_hi, wf_lo, bf_pad, seq)

    nk = seq // tile
    frow = fcum[:, :N_HEADS].reshape(bsz, nk, tile, N_HEADS // 2, 2).transpose(0, 3, 1, 4, 2)

    to3 = lambda t: t.reshape(bsz, seq, D_BRANCH)
    y_fox = _fox_attn(to3(qf), to3(kf), to3(vf), frow, tile).reshape(n, D_BRANCH)
    y_sb = _sb_attn(to3(qs), to3(ks), to3(vs), tile).reshape(n, D_BRANCH)

    wr = _pad_lanes(jnp.concatenate([w_group[0], w_expert_router[0]], axis=1))
    wr_hi, wr_lo = _split2(wr)
    br = _pad_lanes(jnp.concatenate([b_group[0], b_expert_router[0]])[None, :])
    h1, route = _post_attn(
        x2, row(ln_in_g), row(ln_in_b), y_fox, y_sb, gf, gs,
        w_up_fox[0].astype(_bf16), w_up_sb[0].astype(_bf16), w_out[0].astype(_bf16),
        row(ln1_g[0]), row(ln1_b[0]), wr_hi, wr_lo, br)

    expert_id = route[:, 2:4].astype(jnp.int32)
    dest, src3, blk_expert, n_used = _dispatch_plan(expert_id)
    y_pad = _experts(blk_expert, n_used, src3, h1,
                     w1[0].astype(_bf16), w3[0].astype(_bf16), w2[0].astype(_bf16))

    tm = COMBINE_TILE
    pos3 = dest.reshape(n // tm, tm, TOP_K).transpose(0, 2, 1).reshape(n // tm, 1, TOP_K * tm)
    out = _combine(pos3, h1, route, row(ln2_g[0]), row(ln2_b[0]), y_pad)
    return out.reshape(bsz, seq, d)
```

```python
import functools

import jax
import jax.numpy as jnp
from jax import lax
from jax.experimental import pallas as pl
from jax.experimental.pallas import tpu as pltpu

HEAD_DIM = 64
N_HEADS = 8
D_BRANCH = N_HEADS * HEAD_DIM
N_GROUPS = 4
EXPERTS_PER_GROUP = 8
N_EXPERTS = N_GROUPS * EXPERTS_PER_GROUP
TOP_K = 2
DISPATCH_BLOCK = 256
DEPTH = 1
DN_ALPHA = (2.0 * DEPTH) ** 0.25
LN_EPS = 1e-5

LANES = 128
TOKEN_TILE = 512
FOX_TILE = 512
FOX_SAFE_DOT = 60.0
SB_TILE = 256
SB_DEAD = 150.0
LOG2E = 1.4426950408889634
COMBINE_TILE = 256
VMEM_LIMIT = 56 * 1024 * 1024
NEG_BIG = -1e30

_f32 = jnp.float32
_bf16 = jnp.bfloat16


def _layer_norm(x, g, b):
    mu = jnp.mean(x, axis=-1, keepdims=True)
    xc = x - mu
    var = jnp.mean(xc * xc, axis=-1, keepdims=True)
    return xc * lax.rsqrt(var + LN_EPS) * g + b


def _split2(x):
    hi = x.astype(_bf16)
    lo = (x - hi.astype(_f32)).astype(_bf16)
    return hi, lo


def _split3(x):
    hi = x.astype(_bf16)
    r = x - hi.astype(_f32)
    mid = r.astype(_bf16)
    lo = (r - mid.astype(_f32)).astype(_bf16)
    return hi, mid, lo


def _dot(a, b):
    return jnp.dot(a, b, preferred_element_type=_f32)


def _dot_nt(a, b):
    return lax.dot_general(a, b, (((1,), (1,)), ((), ())), preferred_element_type=_f32)


def _inproj_kernel(x_ref, g_ref, b_ref, w_ref, wf_hi_ref, wf_lo_ref, bf_ref, tri_ref,
                   qf_ref, kf_ref, vf_ref, qs_ref, ks_ref, vs_ref, gf_ref, gs_ref, fcum_ref,
                   carry_ref, *, tiles_per_seq):
    i = pl.program_id(0)
    h = _layer_norm(x_ref[...], g_ref[...], b_ref[...])
    h_hi, h_lo = _split2(h)

    col = 0
    for ref in (qf_ref, kf_ref, vf_ref, qs_ref, ks_ref, vs_ref, gf_ref, gs_ref):
        width = ref.shape[-1]
        ref[...] = _dot(h_hi, w_ref[:, col:col + width]).astype(ref.dtype)
        col += width

    f_logit = (_dot(h_hi, wf_hi_ref[...]) + _dot(h_lo, wf_hi_ref[...])
               + _dot(h_hi, wf_lo_ref[...])) + bf_ref[...]
    log_f = jnp.minimum(f_logit, 0.0) - jnp.log(1.0 + jnp.exp(-jnp.abs(f_logit)))

    p0, p1, p2 = _split3(log_f)
    tri = tri_ref[...]
    cum = _dot(tri, p0) + _dot(tri, p1) + _dot(tri, p2)

    @pl.when(i % tiles_per_seq == 0)
    def _():
        carry_ref[...] = jnp.zeros_like(carry_ref)

    cum = cum + carry_ref[...]
    fcum_ref[...] = cum * LOG2E
    carry_ref[...] = cum[-1:, :]


def _inproj(x2, ln_g, ln_b, w_main, wf_hi, wf_lo, bf_pad, seq_len):
    n, d = x2.shape
    tm = TOKEN_TILE
    assert n % tm == 0 and seq_len % tm == 0
    tri = (lax.broadcasted_iota(jnp.int32, (tm, tm), 1)
           <= lax.broadcasted_iota(jnp.int32, (tm, tm), 0)).astype(_bf16)
    const = lambda i: (0, 0)
    row = lambda i: (i, 0)
    widths = (D_BRANCH,) * 6 + (d, d)
    out_shape = [jax.ShapeDtypeStruct((n, w), _bf16) for w in widths]
    out_shape.append(jax.ShapeDtypeStruct((n, LANES), _f32))
    out_specs = [pl.BlockSpec((tm, w), row) for w in widths] + [pl.BlockSpec((tm, LANES), row)]
    return pl.pallas_call(
        functools.partial(_inproj_kernel, tiles_per_seq=seq_len // tm),
        grid=(n // tm,),
        in_specs=[
            pl.BlockSpec((tm, d), row),
            pl.BlockSpec((1, d), const),
            pl.BlockSpec((1, d), const),
            pl.BlockSpec(w_main.shape, const),
            pl.BlockSpec(wf_hi.shape, const),
            pl.BlockSpec(wf_lo.shape, const),
            pl.BlockSpec((1, LANES), const),
            pl.BlockSpec((tm, tm), const),
        ],
        out_specs=out_specs,
        out_shape=out_shape,
        scratch_shapes=[pltpu.VMEM((1, LANES), _f32)],
        compiler_params=pltpu.CompilerParams(
            dimension_semantics=("arbitrary",), vmem_limit_bytes=VMEM_LIMIT),
        name="inproj",
    )(x2, ln_g, ln_b, w_main, wf_hi, wf_lo, bf_pad, tri)


def _head_masks(shape):
    lane = lax.broadcasted_iota(jnp.int32, shape, 1)
    return lane < HEAD_DIM


def _fox_kernel(q_ref, k_ref, v_ref, frow_ref, fcol_ref, o_ref, m0_ref, m1_ref, acc0_ref, acc1_ref, kmax_ref,
                *, tile):
    i = pl.program_id(2)
    nk = pl.num_programs(2)
    heads = range(2)
    q = q_ref[0]
    first_head = _head_masks(q.shape)
    zero_q = jnp.zeros_like(q)
    qh = (jnp.where(first_head, q, zero_q), jnp.where(first_head, zero_q, q))
    fcol = [fcol_ref[0, 0, :, h:h + 1] for h in heads]
    row = lax.broadcasted_iota(jnp.int32, (tile, tile), 0)
    colm = lax.broadcasted_iota(jnp.int32, (tile, tile), 1)
    causal = colm <= row
    m_refs = (m0_ref, m1_ref)
    acc_refs = (acc0_ref, acc1_ref)
    in_first = lax.broadcasted_iota(jnp.int32, (LANES, LANES), 0) < HEAD_DIM
    head_sum = (in_first.astype(_bf16), jnp.logical_not(in_first).astype(_bf16))

    def max_sq_norms(x):
        xf = x.astype(_f32)
        sq = (xf * xf).astype(_bf16)
        return [jnp.max(_dot(sq, head_sum[h]), axis=0, keepdims=True) for h in heads]

    def k_tile(j):
        return k_ref[0, pl.ds(pl.multiple_of(j * tile, tile), tile), :]

    def v_tile(j, h):
        return v_ref[0, pl.ds(pl.multiple_of(j * tile, tile), tile), h * LANES:(h + 1) * LANES]

    @pl.when(i == 0)
    def _():
        def body(c, carry):
            n2 = max_sq_norms(k_tile(c))
            return tuple(jnp.maximum(carry[h], n2[h]) for h in heads)
        zero = jnp.zeros((1, LANES), _f32)
        km = lax.fori_loop(0, nk, body, (zero, zero))
        for h in heads:
            kmax_ref[h:h + 1, :] = km[h]

    for h in heads:
        acc_refs[h][...] = jnp.zeros(acc_refs[h].shape, _f32)

    qn2 = max_sq_norms(q)
    bound_sq = jnp.max(jnp.maximum(qn2[0] * kmax_ref[0:1, :], qn2[1] * kmax_ref[1:2, :])) * 1.05
    safe = bound_sq <= FOX_SAFE_DOT * FOX_SAFE_DOT

    def logits(j):
        kt = k_tile(j)
        return [(_dot_nt(qh[h], kt) - frow_ref[0, 0, j, h:h + 1, :]) + fcol[h] for h in heads]

    def run(step):
        def body(j, carry):
            step(j, False)
            return carry
        lax.fori_loop(0, i, body, 0)
        step(i, True)

    @pl.when(safe)
    def _():
        def step(j, masked):
            p = [jnp.exp2(t) for t in logits(j)]
            if masked:
                p = [jnp.where(causal, ph, 0.0) for ph in p]
            pv = [_dot(p[h].astype(_bf16), v_tile(j, h)) for h in heads]
            for h in heads:
                acc_refs[h][...] += pv[h]
        run(step)

    @pl.when(jnp.logical_not(safe))
    def _():
        for h in heads:
            m_refs[h][...] = jnp.full(m_refs[h].shape, NEG_BIG, _f32)

        def step(j, masked):
            t = logits(j)
            if masked:
                t = [jnp.where(causal, th, NEG_BIG) for th in t]
            m_old = [m_refs[h][...] for h in heads]
            m_new = [jnp.maximum(m_old[h], jnp.max(t[h], axis=-1, keepdims=True)) for h in heads]
            p = [jnp.exp2(t[h] - m_new[h]).astype(_bf16) for h in heads]
            pv = [_dot(p[h], v_tile(j, h)) for h in heads]
            for h in heads:
                acc_refs[h][...] = jnp.exp2(m_old[h] - m_new[h]) * acc_refs[h][...] + pv[h]
                m_refs[h][...] = m_new[h]
        run(step)

    acc0 = acc0_ref[...]
    acc1 = acc1_ref[...]
    res0 = acc0 / acc0[:, HEAD_DIM:HEAD_DIM + 1]
    res1 = acc1 / acc1[:, 0:1]
    o_ref[0] = jnp.where(_head_masks(res0.shape), res0, res1).astype(o_ref.dtype)


def _fox_attn(q, k, v_aug, frow, fcol, tile):
    b, s, _ = q.shape
    n_pairs = D_BRANCH // LANES
    nk = s // tile
    return pl.pallas_call(
        functools.partial(_fox_kernel, tile=tile),
        grid=(b, n_pairs, nk),
        in_specs=[
            pl.BlockSpec((1, tile, LANES), lambda bi, hp, i: (bi, i, hp)),
            pl.BlockSpec((1, s, LANES), lambda bi, hp, i: (bi, 0, hp)),
            pl.BlockSpec((1, s, 2 * LANES), lambda bi, hp, i: (bi, 0, hp)),
            pl.BlockSpec((1, 1, nk, 2, tile), lambda bi, hp, i: (bi, hp, 0, 0, 0)),
            pl.BlockSpec((1, 1, tile, 2), lambda bi, hp, i: (bi, hp, i, 0)),
        ],
        out_specs=pl.BlockSpec((1, tile, LANES), lambda bi, hp, i: (bi, i, hp)),
        out_shape=jax.ShapeDtypeStruct((b, s, D_BRANCH), _bf16),
        scratch_shapes=[pltpu.VMEM((tile, 1), _f32), pltpu.VMEM((tile, 1), _f32),
                        pltpu.VMEM((tile, LANES), _f32), pltpu.VMEM((tile, LANES), _f32),
                        pltpu.VMEM((8, LANES), _f32)],
        compiler_params=pltpu.CompilerParams(
            dimension_semantics=("arbitrary", "arbitrary", "arbitrary"),
            vmem_limit_bytes=VMEM_LIMIT),
        name="fox_attn",
    )(q, k, v_aug, frow, fcol)


def _sb_kernel(q_ref, k_ref, v_ref, tri_ref, o_ref, c0_ref, c1_ref, acc0_ref, acc1_ref, *, tile):
    i = pl.program_id(2)
    q = q_ref[0]
    first_head = _head_masks(q.shape)
    zero = jnp.zeros_like(q)
    qh = (jnp.where(first_head, q, zero), jnp.where(first_head, zero, q))
    row = lax.broadcasted_iota(jnp.int32, (tile, tile), 0)
    colm = lax.broadcasted_iota(jnp.int32, (tile, tile), 1)
    strict = colm < row
    tri2 = tri_ref[...]
    c_refs = (c0_ref, c1_ref)
    acc_refs = (acc0_ref, acc1_ref)
    heads = range(2)
    sign_bit = jnp.uint32(0x80000000)

    def sweep(tiles, c):
        n_t = range(len(tiles))
        kt = [k_ref[0, pl.ds(pl.multiple_of(j * tile, tile), tile), :] for j, _, _ in tiles]
        vt = [v_ref[0, pl.ds(pl.multiple_of(j * tile, tile), tile), :] for j, _, _ in tiles]
        vt = [v if s is None else v * s for v, (_, _, s) in zip(vt, tiles)]
        z = [[_dot_nt(qh[h], kt[u]) for h in heads] for u in n_t]
        neg_abs = [[pltpu.bitcast(pltpu.bitcast(z[u][h], jnp.uint32) | sign_bit, _f32) for h in heads]
                   for u in n_t]
        sp = [[jnp.maximum(z[u][h], 0.0) + jnp.log2(1.0 + jnp.exp2(neg_abs[u][h])) for h in heads]
              for u in n_t]
        sp = [[jnp.where(strict, s, 0.0) for s in sp[u]] if tiles[u][1] else sp[u] for u in n_t]
        incl = [[_dot(jnp.concatenate(_split2(sp[u][h]), axis=1), tri2) for h in heads] for u in n_t]
        pv = [None, None]
        for u in n_t:
            a = [jnp.exp2(z[u][h] - incl[u][h] - c[h]) for h in heads]
            if tiles[u][1]:
                a = [jnp.where(strict, ah, 0.0) for ah in a]
            for h in heads:
                d = _dot(a[h].astype(_bf16), vt[u])
                pv[h] = d if pv[h] is None else pv[h] + d
            c = [c[h] + incl[u][h][:, 0:1] for h in heads]
        return pv, c

    has_prev = (i > 0).astype(_bf16)
    zero_c = jnp.zeros(c0_ref.shape, _f32)
    pv, c = sweep([(i, True, None), (jnp.maximum(i - 1, 0), False, has_prev)], [zero_c, zero_c])
    for h in heads:
        acc_refs[h][...] = pv[h]
        c_refs[h][...] = c[h]

    def cond(carry):
        j, c_min = carry
        return jnp.logical_and(j >= 0, c_min < SB_DEAD)

    def body(carry):
        j, _ = carry
        pv, c = sweep([(j, False, None)], [c_refs[h][...] for h in heads])
        for h in heads:
            acc_refs[h][...] += pv[h]
            c_refs[h][...] = c[h]
        return j - 1, jnp.min(jnp.minimum(c[0], c[1]))

    lax.while_loop(cond, body, (i - 2, jnp.min(jnp.minimum(c[0], c[1]))))
    o_ref[0] = jnp.where(first_head, acc0_ref[...], acc1_ref[...]).astype(o_ref.dtype)


def _sb_attn(q, k, v, tile):
    b, s, _ = q.shape
    n_pairs = D_BRANCH // LANES
    nk = s // tile
    tri = (lax.broadcasted_iota(jnp.int32, (tile, tile), 0)
           >= lax.broadcasted_iota(jnp.int32, (tile, tile), 1)).astype(_bf16)
    tri = jnp.concatenate([tri, tri], axis=0)
    return pl.pallas_call(
        functools.partial(_sb_kernel, tile=tile),
        grid=(b, n_pairs, nk),
        in_specs=[
            pl.BlockSpec((1, tile, LANES), lambda bi, hp, i: (bi, i, hp)),
            pl.BlockSpec((1, s, LANES), lambda bi, hp, i: (bi, 0, hp)),
            pl.BlockSpec((1, s, LANES), lambda bi, hp, i: (bi, 0, hp)),
            pl.BlockSpec((2 * tile, tile), lambda bi, hp, i: (0, 0)),
        ],
        out_specs=pl.BlockSpec((1, tile, LANES), lambda bi, hp, i: (bi, i, hp)),
        out_shape=jax.ShapeDtypeStruct((b, s, D_BRANCH), _bf16),
        scratch_shapes=[pltpu.VMEM((tile, 1), _f32), pltpu.VMEM((tile, 1), _f32),
                        pltpu.VMEM((tile, LANES), _f32), pltpu.VMEM((tile, LANES), _f32)],
        compiler_params=pltpu.CompilerParams(
            dimension_semantics=("arbitrary", "arbitrary", "arbitrary"),
            vmem_limit_bytes=VMEM_LIMIT),
        name="sb_attn",
    )(q, k, v, tri)


def _post_attn_kernel(x_ref, lng_ref, lnb_ref, yf_ref, ys_ref, gf_ref, gs_ref,
                      wuf_ref, wus_ref, wo_ref, g1_ref, b1_ref, wr_hi_ref, wr_lo_ref, br_ref,
                      h1_ref, route_ref):
    h0 = _layer_norm(x_ref[...], lng_ref[...], lnb_ref[...])
    up_f = _dot(yf_ref[...], wuf_ref[...])
    up_s = _dot(ys_ref[...], wus_ref[...])
    sig_f = 1.0 / (1.0 + jnp.exp(-gf_ref[...].astype(_f32)))
    sig_s = 1.0 / (1.0 + jnp.exp(-gs_ref[...].astype(_f32)))
    merged = sig_f * up_f + sig_s * up_s
    mix = _dot(merged.astype(_bf16), wo_ref[...])
    h1 = _layer_norm(DN_ALPHA * h0 + mix, g1_ref[...], b1_ref[...])
    h1_ref[...] = h1

    h_hi, h_lo = _split2(h1)
    logits = (_dot(h_hi, wr_hi_ref[...]) + _dot(h_lo, wr_hi_ref[...])
              + _dot(h_hi, wr_lo_ref[...])) + br_ref[...]
    lane = lax.broadcasted_iota(jnp.int32, logits.shape, 1)
    lane_f = lane.astype(_f32)
    big = float(LANES)

    g_log = jnp.where(lane < N_GROUPS, logits, -jnp.inf)
    g_max = jnp.max(g_log, axis=-1, keepdims=True)
    g_sel = jnp.min(jnp.where(g_log == g_max, lane_f, big), axis=-1, keepdims=True)
    g_gate = 1.0 / jnp.sum(jnp.exp(g_log - g_max), axis=-1, keepdims=True)

    lo_lane = N_GROUPS + g_sel * EXPERTS_PER_GROUP
    in_grp = jnp.logical_and(lane_f >= lo_lane, lane_f < lo_lane + EXPERTS_PER_GROUP)
    e_log = jnp.where(in_grp, logits, -jnp.inf)
    e1 = jnp.max(e_log, axis=-1, keepdims=True)
    i1 = jnp.min(jnp.where(e_log == e1, lane_f, big), axis=-1, keepdims=True)
    e_log2 = jnp.where(lane_f == i1, -jnp.inf, e_log)
    e2 = jnp.max(e_log2, axis=-1, keepdims=True)
    i2 = jnp.min(jnp.where(e_log2 == e2, lane_f, big), axis=-1, keepdims=True)
    d = jnp.exp(e2 - e1)
    w1 = 1.0 / (1.0 + d)
    w2 = d * w1

    route = jnp.where(lane == 0, g_gate * w1, 0.0)
    route = jnp.where(lane == 1, g_gate * w2, route)
    route = jnp.where(lane == 2, i1 - N_GROUPS, route)
    route = jnp.where(lane == 3, i2 - N_GROUPS, route)
    route_ref[...] = route


def _post_attn(x2, lng, lnb, yf, ys, gf, gs, wuf, wus, wo, g1, b1, wr_hi, wr_lo, br):
    n, d = x2.shape
    tm = TOKEN_TILE
    const = lambda i: (0, 0)
    row = lambda i: (i, 0)
    vec = pl.BlockSpec((1, d), const)
    return pl.pallas_call(
        _post_attn_kernel,
        grid=(n // tm,),
        in_specs=[
            pl.BlockSpec((tm, d), row), vec, vec,
            pl.BlockSpec((tm, D_BRANCH), row), pl.BlockSpec((tm, D_BRANCH), row),
            pl.BlockSpec((tm, d), row), pl.BlockSpec((tm, d), row),
            pl.BlockSpec(wuf.shape, const), pl.BlockSpec(wus.shape, const),
            pl.BlockSpec(wo.shape, const), vec, vec,
            pl.BlockSpec(wr_hi.shape, const), pl.BlockSpec(wr_lo.shape, const),
            pl.BlockSpec((1, LANES), const),
        ],
        out_specs=[pl.BlockSpec((tm, d), row), pl.BlockSpec((tm, LANES), row)],
        out_shape=[jax.ShapeDtypeStruct((n, d), _f32), jax.ShapeDtypeStruct((n, LANES), _f32)],
        compiler_params=pltpu.CompilerParams(
            dimension_semantics=("arbitrary",), vmem_limit_bytes=VMEM_LIMIT),
        name="post_attn",
    )(x2, lng, lnb, yf, ys, gf, gs, wuf, wus, wo, g1, b1, wr_hi, wr_lo, br)


def _experts_kernel(blk_expert_ref, n_used_ref, src_ref, src_next_ref, h_hbm, w1_ref, w3_ref, w2_ref,
                    y_ref, buf_ref, sem_ref):
    del blk_expert_ref
    b = pl.program_id(0)
    nb = pl.num_programs(0)
    n_used = n_used_ref[0]
    rows = buf_ref.shape[1]

    def row_copy(idx_ref, r, slot):
        return pltpu.make_async_copy(h_hbm.at[pl.ds(idx_ref[0, 0, r], 1), :],
                                     buf_ref.at[slot, pl.ds(r, 1), :], sem_ref.at[slot])

    def start_block(idx_ref, slot):
        def body(r, carry):
            row_copy(idx_ref, r, slot).start()
            return carry
        lax.fori_loop(0, rows, body, 0)

    def wait_block(idx_ref, slot):
        def body(r, carry):
            row_copy(idx_ref, r, slot).wait()
            return carry
        lax.fori_loop(0, rows, body, 0)

    slot = b % 2

    @pl.when(jnp.logical_and(b == 0, n_used > 0))
    def _():
        start_block(src_ref, 0)

    @pl.when(jnp.logical_and(b + 1 < nb, b + 1 < n_used))
    def _():
        start_block(src_next_ref, 1 - slot)

    @pl.when(b < n_used)
    def _():
        wait_block(src_ref, slot)
        xb = buf_ref[slot].astype(_bf16)
        a = _dot(xb, w1_ref[0])
        g = _dot(xb, w3_ref[0])
        hidden = (a * (1.0 / (1.0 + jnp.exp(-a)))) * g
        y_ref[...] = _dot(hidden.astype(_bf16), w2_ref[0])

    @pl.when(b >= n_used)
    def _():
        y_ref[...] = jnp.zeros_like(y_ref)


def _experts(blk_expert, n_used, src3, h1, w1b, w3b, w2b):
    n_blocks = blk_expert.shape[0]
    rows = DISPATCH_BLOCK
    d = h1.shape[1]
    de = w1b.shape[2]
    grid_spec = pltpu.PrefetchScalarGridSpec(
        num_scalar_prefetch=2,
        grid=(n_blocks,),
        in_specs=[
            pl.BlockSpec((1, 1, rows), lambda b, be, nu: (b, 0, 0), memory_space=pltpu.SMEM),
            pl.BlockSpec((1, 1, rows), lambda b, be, nu: (jnp.minimum(b + 1, n_blocks - 1), 0, 0),
                         memory_space=pltpu.SMEM),
            pl.BlockSpec(memory_space=pl.ANY),
            pl.BlockSpec((1, d, de), lambda b, be, nu: (be[b], 0, 0)),
            pl.BlockSpec((1, d, de), lambda b, be, nu: (be[b], 0, 0)),
            pl.BlockSpec((1, de, d), lambda b, be, nu: (be[b], 0, 0)),
        ],
        out_specs=pl.BlockSpec((rows, d), lambda b, be, nu: (b, 0)),
        scratch_shapes=[pltpu.VMEM((2, rows, d), _f32), pltpu.SemaphoreType.DMA((2,))],
    )
    return pl.pallas_call(
        _experts_kernel,
        grid_spec=grid_spec,
        out_shape=jax.ShapeDtypeStruct((n_blocks * rows, d), _f32),
        compiler_params=pltpu.CompilerParams(
            dimension_semantics=("arbitrary",), vmem_limit_bytes=VMEM_LIMIT),
        name="experts",
    )(blk_expert, n_used, src3, src3, h1, w1b, w3b, w2b)


def _combine_kernel(pos_ref, h1_ref, gate_ref, g2_ref, b2_ref, y_hbm, o_ref, buf_ref, sem_ref):
    tm = o_ref.shape[0]

    def row_copy(r):
        return pltpu.make_async_copy(y_hbm.at[pl.ds(pos_ref[0, 0, r], 1), :],
                                     buf_ref.at[pl.ds(r, 1), :], sem_ref.at[0])

    def start(r, carry):
        row_copy(r).start()
        return carry

    def wait(r, carry):
        row_copy(r).wait()
        return carry

    lax.fori_loop(0, TOP_K * tm, start, 0)
    lax.fori_loop(0, TOP_K * tm, wait, 0)
    gate = gate_ref[...]
    ffn = gate[:, 0:1] * buf_ref[0:tm, :] + gate[:, 1:2] * buf_ref[tm:2 * tm, :]
    o_ref[...] = _layer_norm(DN_ALPHA * h1_ref[...] + ffn, g2_ref[...], b2_ref[...])


def _combine(pos3, h1, route, g2, b2, y_pad):
    n, d = h1.shape
    tm = COMBINE_TILE
    const = lambda i: (0, 0)
    row = lambda i: (i, 0)
    return pl.pallas_call(
        _combine_kernel,
        grid=(n // tm,),
        in_specs=[
            pl.BlockSpec((1, 1, TOP_K * tm), lambda i: (i, 0, 0), memory_space=pltpu.SMEM),
            pl.BlockSpec((tm, d), row),
            pl.BlockSpec((tm, LANES), row),
            pl.BlockSpec((1, d), const), pl.BlockSpec((1, d), const),
            pl.BlockSpec(memory_space=pl.ANY),
        ],
        out_specs=pl.BlockSpec((tm, d), row),
        out_shape=jax.ShapeDtypeStruct((n, d), _f32),
        scratch_shapes=[pltpu.VMEM((TOP_K * tm, d), _f32), pltpu.SemaphoreType.DMA((1,))],
        compiler_params=pltpu.CompilerParams(
            dimension_semantics=("arbitrary",), vmem_limit_bytes=VMEM_LIMIT),
        name="combine",
    )(pos3, h1, route, g2, b2, y_pad)


def _pad_lanes(w):
    return jnp.pad(w, ((0, 0), (0, LANES - w.shape[1])))


def _dispatch_plan(expert_id):
    n = expert_id.shape[0]
    m = n * TOP_K
    flat = expert_id.reshape(m)
    onehot = (flat[:, None] == jnp.arange(N_EXPERTS, dtype=jnp.int32)[None, :]).astype(jnp.int32)
    incl = jnp.cumsum(onehot, axis=0)
    counts = incl[-1]
    rank = jnp.sum((incl - onehot) * onehot, axis=1)
    blocks_per = (counts + DISPATCH_BLOCK - 1) // DISPATCH_BLOCK
    blk_end = jnp.cumsum(blocks_per)
    pstarts = (blk_end - blocks_per) * DISPATCH_BLOCK
    dest = pstarts[flat] + rank
    n_blocks = (m + N_EXPERTS * (DISPATCH_BLOCK - 1) + DISPATCH_BLOCK - 1) // DISPATCH_BLOCK
    n_used = blk_end[-1]
    src = jnp.zeros((n_blocks * DISPATCH_BLOCK,), jnp.int32).at[dest].set(
        jnp.arange(m, dtype=jnp.int32) // TOP_K)
    blk = jnp.arange(n_blocks, dtype=jnp.int32)
    blk_expert = jnp.sum((blk[:, None] >= blk_end[None, :]).astype(jnp.int32), axis=1)
    last_expert = jnp.sum((jnp.maximum(n_used - 1, 0) >= blk_end).astype(jnp.int32))
    blk_expert = jnp.where(blk < n_used, blk_expert, last_expert).astype(jnp.int32)
    return dest.reshape(n, TOP_K), src.reshape(n_blocks, 1, DISPATCH_BLOCK), blk_expert, \
        n_used.astype(jnp.int32).reshape(1)


def kernel(x, ln_in_g, ln_in_b, w_in, b_forget, w_up_fox, w_up_sb, w_out, ln1_g, ln1_b, w_group, b_group,
           w_expert_router, b_expert_router, w1, w3, w2, ln2_g, ln2_b):
    bsz, seq, d = x.shape
    n = bsz * seq
    assert w_in.shape[0] == DEPTH
    x2 = x.reshape(n, d)
    scale = HEAD_DIM ** -0.5

    wi = w_in[0]
    c = D_BRANCH
    off_f = 3 * c
    off_s = off_f + N_HEADS
    off_g = off_s + 3 * c
    w_main = jnp.concatenate([
        wi[:, 0:c] * (scale * LOG2E), wi[:, c:3 * c],
        wi[:, off_s:off_s + c] * (scale * LOG2E), wi[:, off_s + c:off_g],
        wi[:, off_g:],
    ], axis=1).astype(_bf16)
    wf = _pad_lanes(wi[:, off_f:off_s])
    wf_hi, wf_lo = _split2(wf)
    bf_pad = _pad_lanes(b_forget[0][None, :])

    row = lambda v: v.reshape(1, -1)
    qf, kf, vf, qs, ks, vs, gf, gs, fcum = _inproj(
        x2, row(ln_in_g), row(ln_in_b), w_main, wf_hi, wf_lo, bf_pad, seq)

    to3 = lambda t: t.reshape(bsz, seq, D_BRANCH)
    nk = seq // FOX_TILE
    f5 = fcum[:, :N_HEADS].reshape(bsz, nk, FOX_TILE, N_HEADS // 2, 2)
    frow = f5.transpose(0, 3, 1, 4, 2)
    fcol = f5.reshape(bsz, seq, N_HEADS // 2, 2).transpose(0, 2, 1, 3)
    v4 = vf.reshape(n, N_HEADS // 2, 2, HEAD_DIM)
    one = jnp.ones((n, N_HEADS // 2, 1), _bf16)
    pad = jnp.zeros((n, N_HEADS // 2, HEAD_DIM - 1), _bf16)
    vf_aug = jnp.concatenate([v4[:, :, 0], one, pad, one, pad, v4[:, :, 1]], axis=-1)
    vf_aug = vf_aug.reshape(bsz, seq, 2 * D_BRANCH)
    y_fox = _fox_attn(to3(qf), to3(kf), vf_aug, frow, fcol, FOX_TILE).reshape(n, D_BRANCH)
    y_sb = _sb_attn(to3(qs), to3(ks), to3(vs), SB_TILE).reshape(n, D_BRANCH)

    wr = _pad_lanes(jnp.concatenate([w_group[0], w_expert_router[0]], axis=1))
    wr_hi, wr_lo = _split2(wr)
    br = _pad_lanes(jnp.concatenate([b_group[0], b_expert_router[0]])[None, :])
    h1, route = _post_attn(
        x2, row(ln_in_g), row(ln_in_b), y_fox, y_sb, gf, gs,
        w_up_fox[0].astype(_bf16), w_up_sb[0].astype(_bf16), w_out[0].astype(_bf16),
        row(ln1_g[0]), row(ln1_b[0]), wr_hi, wr_lo, br)

    expert_id = route[:, 2:4].astype(jnp.int32)
    dest, src3, blk_expert, n_used = _dispatch_plan(expert_id)
    y_pad = _experts(blk_expert, n_used, src3, h1,
                     w1[0].astype(_bf16), w3[0].astype(_bf16), w2[0].astype(_bf16))

    tm = COMBINE_TILE
    pos3 = dest.reshape(n // tm, tm, TOP_K).transpose(0, 2, 1).reshape(n // tm, 1, TOP_K * tm)
    out = _combine(pos3, h1, route, row(ln2_g[0]), row(ln2_b[0]), y_pad)
    return out.reshape(bsz, seq, d)
```

```python
import functools

import jax
import jax.numpy as jnp
from jax import lax
from jax.experimental import pallas as pl
from jax.experimental.pallas import tpu as pltpu

HEAD_DIM = 64
N_HEADS = 8
D_BRANCH = N_HEADS * HEAD_DIM
N_GROUPS = 4
EXPERTS_PER_GROUP = 8
N_EXPERTS = N_GROUPS * EXPERTS_PER_GROUP
TOP_K = 2
DISPATCH_BLOCK = 256
DEPTH = 1
DN_ALPHA = (2.0 * DEPTH) ** 0.25
LN_EPS = 1e-5

LANES = 128
TOKEN_TILE = 512
FOX_TILE = 512
FOX_SAFE_DOT = 60.0
FOX_DEAD = 150.0
SB_TILE = 256
SB_DEAD = 150.0
LOG2E = 1.4426950408889634
COMBINE_TILE = 256
DMA_UNROLL = 8
VMEM_LIMIT = 56 * 1024 * 1024
NEG_BIG = -1e30

_f32 = jnp.float32
_bf16 = jnp.bfloat16


def _layer_norm(x, g, b):
    mu = jnp.mean(x, axis=-1, keepdims=True)
    xc = x - mu
    var = jnp.mean(xc * xc, axis=-1, keepdims=True)
    return xc * lax.rsqrt(var + LN_EPS) * g + b


def _split2(x):
    hi = x.astype(_bf16)
    lo = (x - hi.astype(_f32)).astype(_bf16)
    return hi, lo


def _split3(x):
    hi = x.astype(_bf16)
    r = x - hi.astype(_f32)
    mid = r.astype(_bf16)
    lo = (r - mid.astype(_f32)).astype(_bf16)
    return hi, mid, lo


def _dot(a, b):
    return jnp.dot(a, b, preferred_element_type=_f32)


def _dot_nt(a, b):
    return lax.dot_general(a, b, (((1,), (1,)), ((), ())), preferred_element_type=_f32)


def _inproj_kernel(x_ref, g_ref, b_ref, w_ref, wf_hi_ref, wf_lo_ref, bf_ref, tri_ref,
                   qf_ref, kf_ref, vf_ref, qs_ref, ks_ref, vs_ref, gf_ref, gs_ref, fcum_ref,
                   carry_ref, *, tiles_per_seq):
    i = pl.program_id(0)
    h = _layer_norm(x_ref[...], g_ref[...], b_ref[...])
    h_hi, h_lo = _split2(h)

    col = 0
    for ref in (qf_ref, kf_ref, None, qs_ref, ks_ref, vs_ref, gf_ref, gs_ref):
        width = D_BRANCH if ref is None else ref.shape[-1]
        res = _dot(h_hi, w_ref[:, col:col + width])
        col += width
        if ref is not None:
            ref[...] = res.astype(ref.dtype)
            continue
        lane = lax.broadcasted_iota(jnp.int32, (res.shape[0], LANES), 1)
        one_at = lambda k: jnp.where(lane == k, 1.0, 0.0)
        for p in range(D_BRANCH // LANES):
            vp = res[:, p * LANES:(p + 1) * LANES]
            first = jnp.where(lane < HEAD_DIM, vp, one_at(HEAD_DIM))
            second = jnp.where(lane < HEAD_DIM, one_at(0), vp)
            vf_ref[:, 2 * p * LANES:(2 * p + 1) * LANES] = first.astype(vf_ref.dtype)
            vf_ref[:, (2 * p + 1) * LANES:(2 * p + 2) * LANES] = second.astype(vf_ref.dtype)

    f_logit = (_dot(h_hi, wf_hi_ref[...]) + _dot(h_lo, wf_hi_ref[...])
               + _dot(h_hi, wf_lo_ref[...])) + bf_ref[...]
    log_f = jnp.minimum(f_logit, 0.0) - jnp.log(1.0 + jnp.exp(-jnp.abs(f_logit)))

    p0, p1, p2 = _split3(log_f)
    tri = tri_ref[...]
    cum = _dot(tri, p0) + _dot(tri, p1) + _dot(tri, p2)

    @pl.when(i % tiles_per_seq == 0)
    def _():
        carry_ref[...] = jnp.zeros_like(carry_ref)

    cum = cum + carry_ref[...]
    fcum_ref[...] = cum * LOG2E
    carry_ref[...] = cum[-1:, :]


def _inproj(x2, ln_g, ln_b, w_main, wf_hi, wf_lo, bf_pad, seq_len):
    n, d = x2.shape
    tm = TOKEN_TILE
    assert n % tm == 0 and seq_len % tm == 0
    tri = (lax.broadcasted_iota(jnp.int32, (tm, tm), 1)
           <= lax.broadcasted_iota(jnp.int32, (tm, tm), 0)).astype(_bf16)
    const = lambda i: (0, 0)
    row = lambda i: (i, 0)
    widths = (D_BRANCH, D_BRANCH, 2 * D_BRANCH) + (D_BRANCH,) * 3 + (d, d)
    out_shape = [jax.ShapeDtypeStruct((n, w), _bf16) for w in widths]
    out_shape.append(jax.ShapeDtypeStruct((n, LANES), _f32))
    out_specs = [pl.BlockSpec((tm, w), row) for w in widths] + [pl.BlockSpec((tm, LANES), row)]
    return pl.pallas_call(
        functools.partial(_inproj_kernel, tiles_per_seq=seq_len // tm),
        grid=(n // tm,),
        in_specs=[
            pl.BlockSpec((tm, d), row),
            pl.BlockSpec((1, d), const),
            pl.BlockSpec((1, d), const),
            pl.BlockSpec(w_main.shape, const),
            pl.BlockSpec(wf_hi.shape, const),
            pl.BlockSpec(wf_lo.shape, const),
            pl.BlockSpec((1, LANES), const),
            pl.BlockSpec((tm, tm), const),
        ],
        out_specs=out_specs,
        out_shape=out_shape,
        scratch_shapes=[pltpu.VMEM((1, LANES), _f32)],
        compiler_params=pltpu.CompilerParams(
            dimension_semantics=("arbitrary",), vmem_limit_bytes=VMEM_LIMIT),
        name="inproj",
    )(x2, ln_g, ln_b, w_main, wf_hi, wf_lo, bf_pad, tri)


def _head_masks(shape):
    lane = lax.broadcasted_iota(jnp.int32, shape, 1)
    return lane < HEAD_DIM


def _fox_kernel(q_ref, k_ref, v_ref, frow_ref, fcol_ref, fedge_ref, o_ref, m0_ref, m1_ref, acc0_ref, acc1_ref,
                kmax_ref, *, tile):
    i = pl.program_id(2)
    nk = pl.num_programs(2)
    heads = range(2)
    q = q_ref[0]
    first_head = _head_masks(q.shape)
    zero_q = jnp.zeros_like(q)
    qh = (jnp.where(first_head, q, zero_q), jnp.where(first_head, zero_q, q))
    fcol = [fcol_ref[0, 0, :, h:h + 1] for h in heads]
    row = lax.broadcasted_iota(jnp.int32, (tile, tile), 0)
    colm = lax.broadcasted_iota(jnp.int32, (tile, tile), 1)
    causal = colm <= row
    m_refs = (m0_ref, m1_ref)
    acc_refs = (acc0_ref, acc1_ref)
    in_first = lax.broadcasted_iota(jnp.int32, (LANES, LANES), 0) < HEAD_DIM
    head_sum = (in_first.astype(_bf16), jnp.logical_not(in_first).astype(_bf16))

    def max_sq_norms(x):
        xf = x.astype(_f32)
        sq = (xf * xf).astype(_bf16)
        return [jnp.max(_dot(sq, head_sum[h]), axis=0, keepdims=True) for h in heads]

    def k_tile(j):
        return k_ref[0, pl.ds(pl.multiple_of(j * tile, tile), tile), :]

    def v_tile(j, h):
        return v_ref[0, pl.ds(pl.multiple_of(j * tile, tile), tile), h * LANES:(h + 1) * LANES]

    @pl.when(i == 0)
    def _():
        def body(c, carry):
            n2 = max_sq_norms(k_tile(c))
            return tuple(jnp.maximum(carry[h], n2[h]) for h in heads)
        zero = jnp.zeros((1, LANES), _f32)
        km = lax.fori_loop(0, nk, body, (zero, zero))
        for h in heads:
            kmax_ref[h:h + 1, :] = km[h]

    for h in heads:
        acc_refs[h][...] = jnp.zeros(acc_refs[h].shape, _f32)

    qn2 = max_sq_norms(q)
    bound_sq = [jnp.max(qn2[h] * kmax_ref[h:h + 1, :]) * 1.05 for h in heads]
    safe = jnp.maximum(bound_sq[0], bound_sq[1]) <= FOX_SAFE_DOT * FOX_SAFE_DOT

    def logits(j):
        kt = k_tile(j)
        return [(_dot_nt(qh[h], kt) - frow_ref[0, 0, j, h:h + 1, :]) + fcol[h] for h in heads]

    @pl.when(safe)
    def _():
        def step(j, masked):
            p = [jnp.exp2(t) for t in logits(j)]
            if masked:
                p = [jnp.where(causal, ph, 0.0) for ph in p]
            pv = [_dot(p[h].astype(_bf16), v_tile(j, h)) for h in heads]
            for h in heads:
                acc_refs[h][...] += pv[h]

        def alive(j):
            res = None
            for h in heads:
                x = fedge_ref[0, 0, 2 + h, jnp.maximum(j, 0)] - fedge_ref[0, 0, h, i] - FOX_DEAD
                a = jnp.logical_or(x <= 0.0, bound_sq[h] >= x * x)
                res = a if res is None else jnp.logical_or(res, a)
            return res

        step(i, True)

        def body(j):
            step(j, False)
            return j - 1

        lax.while_loop(lambda j: jnp.logical_and(j >= 0, alive(j)), body, i - 1)

    def run(step):
        def body(j, carry):
            step(j, False)
            return carry
        lax.fori_loop(0, i, body, 0)
        step(i, True)

    @pl.when(jnp.logical_not(safe))
    def _():
        for h in heads:
            m_refs[h][...] = jnp.full(m_refs[h].shape, NEG_BIG, _f32)

        def step(j, masked):
            t = logits(j)
            if masked:
                t = [jnp.where(causal, th, NEG_BIG) for th in t]
            m_old = [m_refs[h][...] for h in heads]
            m_new = [jnp.maximum(m_old[h], jnp.max(t[h], axis=-1, keepdims=True)) for h in heads]
            p = [jnp.exp2(t[h] - m_new[h]).astype(_bf16) for h in heads]
            pv = [_dot(p[h], v_tile(j, h)) for h in heads]
            for h in heads:
                acc_refs[h][...] = jnp.exp2(m_old[h] - m_new[h]) * acc_refs[h][...] + pv[h]
                m_refs[h][...] = m_new[h]
        run(step)

    acc0 = acc0_ref[...]
    acc1 = acc1_ref[...]
    res0 = acc0 / acc0[:, HEAD_DIM:HEAD_DIM + 1]
    res1 = acc1 / acc1[:, 0:1]
    o_ref[0] = jnp.where(_head_masks(res0.shape), res0, res1).astype(o_ref.dtype)


def _fox_attn(q, k, v_aug, frow, fcol, fedge, tile):
    b, s, _ = q.shape
    n_pairs = D_BRANCH // LANES
    nk = s // tile
    return pl.pallas_call(
        functools.partial(_fox_kernel, tile=tile),
        grid=(b, n_pairs, nk),
        in_specs=[
            pl.BlockSpec((1, tile, LANES), lambda bi, hp, i: (bi, i, hp)),
            pl.BlockSpec((1, s, LANES), lambda bi, hp, i: (bi, 0, hp)),
            pl.BlockSpec((1, s, 2 * LANES), lambda bi, hp, i: (bi, 0, hp)),
            pl.BlockSpec((1, 1, nk, 2, tile), lambda bi, hp, i: (bi, hp, 0, 0, 0)),
            pl.BlockSpec((1, 1, tile, 2), lambda bi, hp, i: (bi, hp, i, 0)),
            pl.BlockSpec((1, 1, 4, nk), lambda bi, hp, i: (bi, hp, 0, 0), memory_space=pltpu.SMEM),
        ],
        out_specs=pl.BlockSpec((1, tile, LANES), lambda bi, hp, i: (bi, i, hp)),
        out_shape=jax.ShapeDtypeStruct((b, s, D_BRANCH), _bf16),
        scratch_shapes=[pltpu.VMEM((tile, 1), _f32), pltpu.VMEM((tile, 1), _f32),
                        pltpu.VMEM((tile, LANES), _f32), pltpu.VMEM((tile, LANES), _f32),
                        pltpu.VMEM((8, LANES), _f32)],
        compiler_params=pltpu.CompilerParams(
            dimension_semantics=("arbitrary", "arbitrary", "arbitrary"),
            vmem_limit_bytes=VMEM_LIMIT),
        name="fox_attn",
    )(q, k, v_aug, frow, fcol, fedge)


def _sb_kernel(q_ref, k_ref, v_ref, tri_ref, o_ref, c0_ref, c1_ref, acc0_ref, acc1_ref, *, tile):
    i = pl.program_id(2)
    q = q_ref[0]
    first_head = _head_masks(q.shape)
    zero = jnp.zeros_like(q)
    qh = (jnp.where(first_head, q, zero), jnp.where(first_head, zero, q))
    row = lax.broadcasted_iota(jnp.int32, (tile, tile), 0)
    colm = lax.broadcasted_iota(jnp.int32, (tile, tile), 1)
    strict = colm < row
    tri2 = tri_ref[...]
    c_refs = (c0_ref, c1_ref)
    acc_refs = (acc0_ref, acc1_ref)
    heads = range(2)
    sign_bit = jnp.uint32(0x80000000)

    def sweep(tiles, c):
        n_t = range(len(tiles))
        kt = [k_ref[0, pl.ds(pl.multiple_of(j * tile, tile), tile), :] for j, _, _ in tiles]
        vt = [v_ref[0, pl.ds(pl.multiple_of(j * tile, tile), tile), :] for j, _, _ in tiles]
        vt = [v if s is None else v * s for v, (_, _, s) in zip(vt, tiles)]
        z = [[_dot_nt(qh[h], kt[u]) for h in heads] for u in n_t]
        neg_abs = [[pltpu.bitcast(pltpu.bitcast(z[u][h], jnp.uint32) | sign_bit, _f32) for h in heads]
                   for u in n_t]
        sp = [[jnp.maximum(z[u][h], 0.0) + jnp.log2(1.0 + jnp.exp2(neg_abs[u][h])) for h in heads]
              for u in n_t]
        sp = [[jnp.where(strict, s, 0.0) for s in sp[u]] if tiles[u][1] else sp[u] for u in n_t]
        incl = [[_dot(jnp.concatenate(_split2(sp[u][h]), axis=1), tri2) for h in heads] for u in n_t]
        pv = [None, None]
        for u in n_t:
            a = [jnp.exp2(z[u][h] - incl[u][h] - c[h]) for h in heads]
            if tiles[u][1]:
                a = [jnp.where(strict, ah, 0.0) for ah in a]
            for h in heads:
                d = _dot(a[h].astype(_bf16), vt[u])
                pv[h] = d if pv[h] is None else pv[h] + d
            c = [c[h] + incl[u][h][:, 0:1] for h in heads]
        return pv, c

    has_prev = (i > 0).astype(_bf16)
    zero_c = jnp.zeros(c0_ref.shape, _f32)
    pv, c = sweep([(i, True, None), (jnp.maximum(i - 1, 0), False, has_prev)], [zero_c, zero_c])
    for h in heads:
        acc_refs[h][...] = pv[h]
        c_refs[h][...] = c[h]

    def cond(carry):
        j, c_min = carry
        return jnp.logical_and(j >= 0, c_min < SB_DEAD)

    def body(carry):
        j, _ = carry
        pv, c = sweep([(j, False, None)], [c_refs[h][...] for h in heads])
        for h in heads:
            acc_refs[h][...] += pv[h]
            c_refs[h][...] = c[h]
        return j - 1, jnp.min(jnp.minimum(c[0], c[1]))

    lax.while_loop(cond, body, (i - 2, jnp.min(jnp.minimum(c[0], c[1]))))
    o_ref[0] = jnp.where(first_head, acc0_ref[...], acc1_ref[...]).astype(o_ref.dtype)


def _sb_attn(q, k, v, tile):
    b, s, _ = q.shape
    n_pairs = D_BRANCH // LANES
    nk = s // tile
    tri = (lax.broadcasted_iota(jnp.int32, (tile, tile), 0)
           >= lax.broadcasted_iota(jnp.int32, (tile, tile), 1)).astype(_bf16)
    tri = jnp.concatenate([tri, tri], axis=0)
    return pl.pallas_call(
        functools.partial(_sb_kernel, tile=tile),
        grid=(b, n_pairs, nk),
        in_specs=[
            pl.BlockSpec((1, tile, LANES), lambda bi, hp, i: (bi, i, hp)),
            pl.BlockSpec((1, s, LANES), lambda bi, hp, i: (bi, 0, hp)),
            pl.BlockSpec((1, s, LANES), lambda bi, hp, i: (bi, 0, hp)),
            pl.BlockSpec((2 * tile, tile), lambda bi, hp, i: (0, 0)),
        ],
        out_specs=pl.BlockSpec((1, tile, LANES), lambda bi, hp, i: (bi, i, hp)),
        out_shape=jax.ShapeDtypeStruct((b, s, D_BRANCH), _bf16),
        scratch_shapes=[pltpu.VMEM((tile, 1), _f32), pltpu.VMEM((tile, 1), _f32),
                        pltpu.VMEM((tile, LANES), _f32), pltpu.VMEM((tile, LANES), _f32)],
        compiler_params=pltpu.CompilerParams(
            dimension_semantics=("arbitrary", "arbitrary", "arbitrary"),
            vmem_limit_bytes=VMEM_LIMIT),
        name="sb_attn",
    )(q, k, v, tri)


def _post_attn_kernel(x_ref, lng_ref, lnb_ref, yf_ref, ys_ref, gf_ref, gs_ref,
                      wuf_ref, wus_ref, wo_ref, g1_ref, b1_ref, wr_hi_ref, wr_lo_ref, br_ref,
                      h1_ref, route_ref):
    h0 = _layer_norm(x_ref[...], lng_ref[...], lnb_ref[...])
    up_f = _dot(yf_ref[...], wuf_ref[...])
    up_s = _dot(ys_ref[...], wus_ref[...])
    sig_f = 1.0 / (1.0 + jnp.exp(-gf_ref[...].astype(_f32)))
    sig_s = 1.0 / (1.0 + jnp.exp(-gs_ref[...].astype(_f32)))
    merged = sig_f * up_f + sig_s * up_s
    mix = _dot(merged.astype(_bf16), wo_ref[...])
    h1 = _layer_norm(DN_ALPHA * h0 + mix, g1_ref[...], b1_ref[...])
    h1_ref[...] = h1

    h_hi, h_lo = _split2(h1)
    logits = (_dot(h_hi, wr_hi_ref[...]) + _dot(h_lo, wr_hi_ref[...])
              + _dot(h_hi, wr_lo_ref[...])) + br_ref[...]
    lane = lax.broadcasted_iota(jnp.int32, logits.shape, 1)
    lane_f = lane.astype(_f32)
    big = float(LANES)

    g_log = jnp.where(lane < N_GROUPS, logits, -jnp.inf)
    g_max = jnp.max(g_log, axis=-1, keepdims=True)
    g_sel = jnp.min(jnp.where(g_log == g_max, lane_f, big), axis=-1, keepdims=True)
    g_gate = 1.0 / jnp.sum(jnp.exp(g_log - g_max), axis=-1, keepdims=True)

    lo_lane = N_GROUPS + g_sel * EXPERTS_PER_GROUP
    in_grp = jnp.logical_and(lane_f >= lo_lane, lane_f < lo_lane + EXPERTS_PER_GROUP)
    e_log = jnp.where(in_grp, logits, -jnp.inf)
    e1 = jnp.max(e_log, axis=-1, keepdims=True)
    i1 = jnp.min(jnp.where(e_log == e1, lane_f, big), axis=-1, keepdims=True)
    e_log2 = jnp.where(lane_f == i1, -jnp.inf, e_log)
    e2 = jnp.max(e_log2, axis=-1, keepdims=True)
    i2 = jnp.min(jnp.where(e_log2 == e2, lane_f, big), axis=-1, keepdims=True)
    d = jnp.exp(e2 - e1)
    w1 = 1.0 / (1.0 + d)
    w2 = d * w1

    route = jnp.where(lane == 0, g_gate * w1, 0.0)
    route = jnp.where(lane == 1, g_gate * w2, route)
    route = jnp.where(lane == 2, i1 - N_GROUPS, route)
    route = jnp.where(lane == 3, i2 - N_GROUPS, route)
    route_ref[...] = route


def _post_attn(x2, lng, lnb, yf, ys, gf, gs, wuf, wus, wo, g1, b1, wr_hi, wr_lo, br):
    n, d = x2.shape
    tm = TOKEN_TILE
    const = lambda i: (0, 0)
    row = lambda i: (i, 0)
    vec = pl.BlockSpec((1, d), const)
    return pl.pallas_call(
        _post_attn_kernel,
        grid=(n // tm,),
        in_specs=[
            pl.BlockSpec((tm, d), row), vec, vec,
            pl.BlockSpec((tm, D_BRANCH), row), pl.BlockSpec((tm, D_BRANCH), row),
            pl.BlockSpec((tm, d), row), pl.BlockSpec((tm, d), row),
            pl.BlockSpec(wuf.shape, const), pl.BlockSpec(wus.shape, const),
            pl.BlockSpec(wo.shape, const), vec, vec,
            pl.BlockSpec(wr_hi.shape, const), pl.BlockSpec(wr_lo.shape, const),
            pl.BlockSpec((1, LANES), const),
        ],
        out_specs=[pl.BlockSpec((tm, d), row), pl.BlockSpec((tm, LANES), row)],
        out_shape=[jax.ShapeDtypeStruct((n, d), _f32), jax.ShapeDtypeStruct((n, LANES), _f32)],
        compiler_params=pltpu.CompilerParams(
            dimension_semantics=("arbitrary",), vmem_limit_bytes=VMEM_LIMIT),
        name="post_attn",
    )(x2, lng, lnb, yf, ys, gf, gs, wuf, wus, wo, g1, b1, wr_hi, wr_lo, br)


def _experts_kernel(blk_expert_ref, n_used_ref, src_ref, src_next_ref, h_hbm, w1_ref, w3_ref, w2_ref,
                    y_ref, buf_ref, sem_ref):
    del blk_expert_ref
    b = pl.program_id(0)
    nb = pl.num_programs(0)
    n_used = n_used_ref[0]
    rows = buf_ref.shape[1]

    def row_copy(idx_ref, r, slot):
        return pltpu.make_async_copy(h_hbm.at[pl.ds(idx_ref[0, 0, r], 1), :],
                                     buf_ref.at[slot, pl.ds(r, 1), :], sem_ref.at[slot])

    def start_block(idx_ref, slot):
        def body(r, carry):
            for u in range(DMA_UNROLL):
                row_copy(idx_ref, r * DMA_UNROLL + u, slot).start()
            return carry
        lax.fori_loop(0, rows // DMA_UNROLL, body, 0)

    def wait_block(idx_ref, slot):
        del idx_ref
        pltpu.make_async_copy(buf_ref.at[slot], buf_ref.at[slot], sem_ref.at[slot]).wait()

    slot = b % 2

    @pl.when(jnp.logical_and(b == 0, n_used > 0))
    def _():
        start_block(src_ref, 0)

    @pl.when(jnp.logical_and(b + 1 < nb, b + 1 < n_used))
    def _():
        start_block(src_next_ref, 1 - slot)

    @pl.when(b < n_used)
    def _():
        wait_block(src_ref, slot)
        xb = buf_ref[slot].astype(_bf16)
        a = _dot(xb, w1_ref[0])
        g = _dot(xb, w3_ref[0])
        hidden = (a * (1.0 / (1.0 + jnp.exp(-a)))) * g
        y_ref[...] = _dot(hidden.astype(_bf16), w2_ref[0])

    @pl.when(b >= n_used)
    def _():
        y_ref[...] = jnp.zeros_like(y_ref)


def _experts(blk_expert, n_used, src3, h1, w1b, w3b, w2b):
    n_blocks = blk_expert.shape[0]
    rows = DISPATCH_BLOCK
    d = h1.shape[1]
    de = w1b.shape[2]
    grid_spec = pltpu.PrefetchScalarGridSpec(
        num_scalar_prefetch=2,
        grid=(n_blocks,),
        in_specs=[
            pl.BlockSpec((1, 1, rows), lambda b, be, nu: (b, 0, 0), memory_space=pltpu.SMEM),
            pl.BlockSpec((1, 1, rows), lambda b, be, nu: (jnp.minimum(b + 1, n_blocks - 1), 0, 0),
                         memory_space=pltpu.SMEM),
            pl.BlockSpec(memory_space=pl.ANY),
            pl.BlockSpec((1, d, de), lambda b, be, nu: (be[b], 0, 0)),
            pl.BlockSpec((1, d, de), lambda b, be, nu: (be[b], 0, 0)),
            pl.BlockSpec((1, de, d), lambda b, be, nu: (be[b], 0, 0)),
        ],
        out_specs=pl.BlockSpec((rows, d), lambda b, be, nu: (b, 0)),
        scratch_shapes=[pltpu.VMEM((2, rows, d), _f32), pltpu.SemaphoreType.DMA((2,))],
    )
    return pl.pallas_call(
        _experts_kernel,
        grid_spec=grid_spec,
        out_shape=jax.ShapeDtypeStruct((n_blocks * rows, d), _f32),
        compiler_params=pltpu.CompilerParams(
            dimension_semantics=("arbitrary",), vmem_limit_bytes=VMEM_LIMIT),
        name="experts",
    )(blk_expert, n_used, src3, src3, h1, w1b, w3b, w2b)


def _combine_kernel(pos_ref, pos_next_ref, h1_ref, gate_ref, g2_ref, b2_ref, y_hbm, o_ref, buf_ref, sem_ref):
    tm = o_ref.shape[0]
    i = pl.program_id(0)
    n_steps = pl.num_programs(0)
    slot = i % 2

    def start_tile(idx_ref, slot):
        def body(r, carry):
            for u in range(DMA_UNROLL):
                rr = r * DMA_UNROLL + u
                pltpu.make_async_copy(y_hbm.at[pl.ds(idx_ref[0, 0, rr], 1), :],
                                      buf_ref.at[slot, pl.ds(rr, 1), :], sem_ref.at[slot]).start()
            return carry
        lax.fori_loop(0, TOP_K * tm // DMA_UNROLL, body, 0)

    @pl.when(i == 0)
    def _():
        start_tile(pos_ref, 0)

    @pl.when(i + 1 < n_steps)
    def _():
        start_tile(pos_next_ref, 1 - slot)

    pltpu.make_async_copy(buf_ref.at[slot], buf_ref.at[slot], sem_ref.at[slot]).wait()
    gate = gate_ref[...]
    ffn = gate[:, 0:1] * buf_ref[slot, 0:tm, :] + gate[:, 1:2] * buf_ref[slot, tm:2 * tm, :]
    o_ref[...] = _layer_norm(DN_ALPHA * h1_ref[...] + ffn, g2_ref[...], b2_ref[...])


def _combine(pos3, h1, route, g2, b2, y_pad):
    n, d = h1.shape
    tm = COMBINE_TILE
    const = lambda i: (0, 0)
    row = lambda i: (i, 0)
    return pl.pallas_call(
        _combine_kernel,
        grid=(n // tm,),
        in_specs=[
            pl.BlockSpec((1, 1, TOP_K * tm), lambda i: (i, 0, 0), memory_space=pltpu.SMEM),
            pl.BlockSpec((1, 1, TOP_K * tm), lambda i: (jnp.minimum(i + 1, n // tm - 1), 0, 0),
                         memory_space=pltpu.SMEM),
            pl.BlockSpec((tm, d), row),
            pl.BlockSpec((tm, LANES), row),
            pl.BlockSpec((1, d), const), pl.BlockSpec((1, d), const),
            pl.BlockSpec(memory_space=pl.ANY),
        ],
        out_specs=pl.BlockSpec((tm, d), row),
        out_shape=jax.ShapeDtypeStruct((n, d), _f32),
        scratch_shapes=[pltpu.VMEM((2, TOP_K * tm, d), _f32), pltpu.SemaphoreType.DMA((2,))],
        compiler_params=pltpu.CompilerParams(
            dimension_semantics=("arbitrary",), vmem_limit_bytes=VMEM_LIMIT),
        name="combine",
    )(pos3, pos3, h1, route, g2, b2, y_pad)


def _pad_lanes(w):
    return jnp.pad(w, ((0, 0), (0, LANES - w.shape[1])))


def _dispatch_plan(expert_id):
    n = expert_id.shape[0]
    m = n * TOP_K
    flat = expert_id.reshape(m)
    onehot = (flat[:, None] == jnp.arange(N_EXPERTS, dtype=jnp.int32)[None, :]).astype(jnp.int32)
    incl = jnp.cumsum(onehot, axis=0)
    counts = incl[-1]
    rank = jnp.sum((incl - onehot) * onehot, axis=1)
    blocks_per = (counts + DISPATCH_BLOCK - 1) // DISPATCH_BLOCK
    blk_end = jnp.cumsum(blocks_per)
    pstarts = (blk_end - blocks_per) * DISPATCH_BLOCK
    dest = pstarts[flat] + rank
    n_blocks = (m + N_EXPERTS * (DISPATCH_BLOCK - 1) + DISPATCH_BLOCK - 1) // DISPATCH_BLOCK
    n_used = blk_end[-1]
    src = jnp.zeros((n_blocks * DISPATCH_BLOCK,), jnp.int32).at[dest].set(
        jnp.arange(m, dtype=jnp.int32) // TOP_K)
    blk = jnp.arange(n_blocks, dtype=jnp.int32)
    blk_expert = jnp.sum((blk[:, None] >= blk_end[None, :]).astype(jnp.int32), axis=1)
    last_expert = jnp.sum((jnp.maximum(n_used - 1, 0) >= blk_end).astype(jnp.int32))
    blk_expert = jnp.where(blk < n_used, blk_expert, last_expert).astype(jnp.int32)
    return dest.reshape(n, TOP_K), src.reshape(n_blocks, 1, DISPATCH_BLOCK), blk_expert, \
        n_used.astype(jnp.int32).reshape(1)


def kernel(x, ln_in_g, ln_in_b, w_in, b_forget, w_up_fox, w_up_sb, w_out, ln1_g, ln1_b, w_group, b_group,
           w_expert_router, b_expert_router, w1, w3, w2, ln2_g, ln2_b):
    bsz, seq, d = x.shape
    n = bsz * seq
    assert w_in.shape[0] == DEPTH
    x2 = x.reshape(n, d)
    scale = HEAD_DIM ** -0.5

    wi = w_in[0]
    c = D_BRANCH
    off_f = 3 * c
    off_s = off_f + N_HEADS
    off_g = off_s + 3 * c
    w_main = jnp.concatenate([
        wi[:, 0:c] * (scale * LOG2E), wi[:, c:3 * c],
        wi[:, off_s:off_s + c] * (scale * LOG2E), wi[:, off_s + c:off_g],
        wi[:, off_g:],
    ], axis=1).astype(_bf16)
    wf = _pad_lanes(wi[:, off_f:off_s])
    wf_hi, wf_lo = _split2(wf)
    bf_pad = _pad_lanes(b_forget[0][None, :])

    row = lambda v: v.reshape(1, -1)
    qf, kf, vf, qs, ks, vs, gf, gs, fcum = _inproj(
        x2, row(ln_in_g), row(ln_in_b), w_main, wf_hi, wf_lo, bf_pad, seq)

    to3 = lambda t: t.reshape(bsz, seq, D_BRANCH)
    nk = seq // FOX_TILE
    f5 = fcum[:, :N_HEADS].reshape(bsz, nk, FOX_TILE, N_HEADS // 2, 2)
    frow = f5.transpose(0, 3, 1, 4, 2)
    fcol = f5.reshape(bsz, seq, N_HEADS // 2, 2).transpose(0, 2, 1, 3)
    fedge = jnp.concatenate([f5[:, :, 0], f5[:, :, FOX_TILE - 1]], axis=-1)
    fedge = fedge.transpose(0, 2, 3, 1)
    vf_aug = vf.reshape(bsz, seq, 2 * D_BRANCH)
    y_fox = _fox_attn(to3(qf), to3(kf), vf_aug, frow, fcol, fedge, FOX_TILE).reshape(n, D_BRANCH)
    y_sb = _sb_attn(to3(qs), to3(ks), to3(vs), SB_TILE).reshape(n, D_BRANCH)

    wr = _pad_lanes(jnp.concatenate([w_group[0], w_expert_router[0]], axis=1))
    wr_hi, wr_lo = _split2(wr)
    br = _pad_lanes(jnp.concatenate([b_group[0], b_expert_router[0]])[None, :])
    h1, route = _post_attn(
        x2, row(ln_in_g), row(ln_in_b), y_fox, y_sb, gf, gs,
        w_up_fox[0].astype(_bf16), w_up_sb[0].astype(_bf16), w_out[0].astype(_bf16),
        row(ln1_g[0]), row(ln1_b[0]), wr_hi, wr_lo, br)

    expert_id = route[:, 2:4].astype(jnp.int32)
    dest, src3, blk_expert, n_used = _dispatch_plan(expert_id)
    y_pad = _experts(blk_expert, n_used, src3, h1,
                     w1[0].astype(_bf16), w3[0].astype(_bf16), w2[0].astype(_bf16))

    tm = COMBINE_TILE
    pos3 = dest.reshape(n // tm, tm, TOP_K).transpose(0, 2, 1).reshape(n // tm, 1, TOP_K * tm)
    out = _combine(pos3, h1, route, row(ln2_g[0]), row(ln2_b[0]), y_pad)
    return out.reshape(bsz, seq, d)
```

```python
import functools

import jax
import jax.numpy as jnp
from jax import lax
from jax.experimental import pallas as pl
from jax.experimental.pallas import tpu as pltpu

HEAD_DIM = 64
N_HEADS = 8
D_BRANCH = N_HEADS * HEAD_DIM
N_GROUPS = 4
EXPERTS_PER_GROUP = 8
N_EXPERTS = N_GROUPS * EXPERTS_PER_GROUP
TOP_K = 2
DISPATCH_BLOCK = 256
DEPTH = 1
DN_ALPHA = (2.0 * DEPTH) ** 0.25
LN_EPS = 1e-5

LANES = 128
TOKEN_TILE = 512
FOX_TILE = 512
FOX_SAFE_DOT = 60.0
FOX_DEAD = 150.0
SB_TILE = 256
SB_DEAD = 150.0
LOG2E = 1.4426950408889634
COMBINE_TILE = 256
DMA_UNROLL = 8
VMEM_LIMIT = 56 * 1024 * 1024
NEG_BIG = -1e30

_f32 = jnp.float32
_bf16 = jnp.bfloat16


def _layer_norm(x, g, b):
    mu = jnp.mean(x, axis=-1, keepdims=True)
    xc = x - mu
    var = jnp.mean(xc * xc, axis=-1, keepdims=True)
    return xc * lax.rsqrt(var + LN_EPS) * g + b


def _split2(x):
    hi = x.astype(_bf16)
    lo = (x - hi.astype(_f32)).astype(_bf16)
    return hi, lo


def _split3(x):
    hi = x.astype(_bf16)
    r = x - hi.astype(_f32)
    mid = r.astype(_bf16)
    lo = (r - mid.astype(_f32)).astype(_bf16)
    return hi, mid, lo


def _dot(a, b):
    return jnp.dot(a, b, preferred_element_type=_f32)


def _dot_nt(a, b):
    return lax.dot_general(a, b, (((1,), (1,)), ((), ())), preferred_element_type=_f32)


def _inproj_kernel(x_ref, g_ref, b_ref, w_ref, wf_hi_ref, wf_lo_ref, bf_ref, tri_ref,
                   qf_ref, kf_ref, vf_ref, qs_ref, ks_ref, vs_ref, gf_ref, gs_ref, fcum_ref,
                   carry_ref, *, tiles_per_seq):
    i = pl.program_id(0)
    h = _layer_norm(x_ref[...], g_ref[...], b_ref[...])
    h_hi, h_lo = _split2(h)

    col = 0
    for ref in (qf_ref, kf_ref, None, qs_ref, ks_ref, vs_ref, gf_ref, gs_ref):
        width = D_BRANCH if ref is None else ref.shape[-1]
        res = _dot(h_hi, w_ref[:, col:col + width])
        col += width
        if ref is not None:
            ref[...] = res.astype(ref.dtype)
            continue
        lane = lax.broadcasted_iota(jnp.int32, (res.shape[0], LANES), 1)
        one_at = lambda k: jnp.where(lane == k, 1.0, 0.0)
        for p in range(D_BRANCH // LANES):
            vp = res[:, p * LANES:(p + 1) * LANES]
            first = jnp.where(lane < HEAD_DIM, vp, one_at(HEAD_DIM))
            second = jnp.where(lane < HEAD_DIM, one_at(0), vp)
            vf_ref[:, 2 * p * LANES:(2 * p + 1) * LANES] = first.astype(vf_ref.dtype)
            vf_ref[:, (2 * p + 1) * LANES:(2 * p + 2) * LANES] = second.astype(vf_ref.dtype)

    f_logit = (_dot(h_hi, wf_hi_ref[...]) + _dot(h_lo, wf_hi_ref[...])
               + _dot(h_hi, wf_lo_ref[...])) + bf_ref[...]
    log_f = jnp.minimum(f_logit, 0.0) - jnp.log(1.0 + jnp.exp(-jnp.abs(f_logit)))

    p0, p1, p2 = _split3(log_f)
    tri = tri_ref[...]
    cum = _dot(tri, p0) + _dot(tri, p1) + _dot(tri, p2)

    @pl.when(i % tiles_per_seq == 0)
    def _():
        carry_ref[...] = jnp.zeros_like(carry_ref)

    cum = cum + carry_ref[...]
    fcum_ref[...] = cum * LOG2E
    carry_ref[...] = cum[-1:, :]


def _inproj(x2, ln_g, ln_b, w_main, wf_hi, wf_lo, bf_pad, seq_len):
    n, d = x2.shape
    tm = TOKEN_TILE
    assert n % tm == 0 and seq_len % tm == 0
    tri = (lax.broadcasted_iota(jnp.int32, (tm, tm), 1)
           <= lax.broadcasted_iota(jnp.int32, (tm, tm), 0)).astype(_bf16)
    const = lambda i: (0, 0)
    row = lambda i: (i, 0)
    widths = (D_BRANCH, D_BRANCH, 2 * D_BRANCH) + (D_BRANCH,) * 3 + (d, d)
    out_shape = [jax.ShapeDtypeStruct((n, w), _bf16) for w in widths]
    out_shape.append(jax.ShapeDtypeStruct((n, LANES), _f32))
    out_specs = [pl.BlockSpec((tm, w), row) for w in widths] + [pl.BlockSpec((tm, LANES), row)]
    return pl.pallas_call(
        functools.partial(_inproj_kernel, tiles_per_seq=seq_len // tm),
        grid=(n // tm,),
        in_specs=[
            pl.BlockSpec((tm, d), row),
            pl.BlockSpec((1, d), const),
            pl.BlockSpec((1, d), const),
            pl.BlockSpec(w_main.shape, const),
            pl.BlockSpec(wf_hi.shape, const),
            pl.BlockSpec(wf_lo.shape, const),
            pl.BlockSpec((1, LANES), const),
            pl.BlockSpec((tm, tm), const),
        ],
        out_specs=out_specs,
        out_shape=out_shape,
        scratch_shapes=[pltpu.VMEM((1, LANES), _f32)],
        compiler_params=pltpu.CompilerParams(
            dimension_semantics=("arbitrary",), vmem_limit_bytes=VMEM_LIMIT),
        name="inproj",
    )(x2, ln_g, ln_b, w_main, wf_hi, wf_lo, bf_pad, tri)


def _head_masks(shape):
    lane = lax.broadcasted_iota(jnp.int32, shape, 1)
    return lane < HEAD_DIM


def _fox_kernel(q_ref, k_ref, v_ref, frow_ref, fcol_ref, fedge_ref, o_ref, m0_ref, m1_ref, acc0_ref, acc1_ref,
                kmax_ref, *, tile):
    i = pl.program_id(2)
    nk = pl.num_programs(2)
    heads = range(2)
    q = q_ref[0]
    first_head = _head_masks(q.shape)
    zero_q = jnp.zeros_like(q)
    qh = (jnp.where(first_head, q, zero_q), jnp.where(first_head, zero_q, q))
    fcol = [fcol_ref[0, 0, :, h:h + 1] for h in heads]
    row = lax.broadcasted_iota(jnp.int32, (tile, tile), 0)
    colm = lax.broadcasted_iota(jnp.int32, (tile, tile), 1)
    causal = colm <= row
    m_refs = (m0_ref, m1_ref)
    acc_refs = (acc0_ref, acc1_ref)
    in_first = lax.broadcasted_iota(jnp.int32, (LANES, LANES), 0) < HEAD_DIM
    head_sum = (in_first.astype(_bf16), jnp.logical_not(in_first).astype(_bf16))

    def max_sq_norms(x):
        xf = x.astype(_f32)
        sq = (xf * xf).astype(_bf16)
        return [jnp.max(_dot(sq, head_sum[h]), axis=0, keepdims=True) for h in heads]

    def k_tile(j):
        return k_ref[0, pl.ds(pl.multiple_of(j * tile, tile), tile), :]

    def v_tile(j, h):
        return v_ref[0, pl.ds(pl.multiple_of(j * tile, tile), tile), h * LANES:(h + 1) * LANES]

    @pl.when(i == 0)
    def _():
        def body(c, carry):
            n2 = max_sq_norms(k_tile(c))
            return tuple(jnp.maximum(carry[h], n2[h]) for h in heads)
        zero = jnp.zeros((1, LANES), _f32)
        km = lax.fori_loop(0, nk, body, (zero, zero))
        for h in heads:
            kmax_ref[h:h + 1, :] = km[h]

    for h in heads:
        acc_refs[h][...] = jnp.zeros(acc_refs[h].shape, _f32)

    qn2 = max_sq_norms(q)
    bound_sq = [jnp.max(qn2[h] * kmax_ref[h:h + 1, :]) * 1.05 for h in heads]
    safe = jnp.maximum(bound_sq[0], bound_sq[1]) <= FOX_SAFE_DOT * FOX_SAFE_DOT

    def logits(j):
        kt = k_tile(j)
        return [(_dot_nt(qh[h], kt) - frow_ref[0, 0, j, h:h + 1, :]) + fcol[h] for h in heads]

    @pl.when(safe)
    def _():
        def step(j, masked):
            p = [jnp.exp2(t) for t in logits(j)]
            if masked:
                p = [jnp.where(causal, ph, 0.0) for ph in p]
            pv = [_dot(p[h].astype(_bf16), v_tile(j, h)) for h in heads]
            for h in heads:
                acc_refs[h][...] += pv[h]

        def alive(j):
            res = None
            for h in heads:
                x = fedge_ref[0, 0, 2 + h, jnp.maximum(j, 0)] - fedge_ref[0, 0, h, i] - FOX_DEAD
                a = jnp.logical_or(x <= 0.0, bound_sq[h] >= x * x)
                res = a if res is None else jnp.logical_or(res, a)
            return res

        step(i, True)

        def body(j):
            step(j, False)
            return j - 1

        lax.while_loop(lambda j: jnp.logical_and(j >= 0, alive(j)), body, i - 1)

    def run(step):
        def body(j, carry):
            step(j, False)
            return carry
        lax.fori_loop(0, i, body, 0)
        step(i, True)

    @pl.when(jnp.logical_not(safe))
    def _():
        for h in heads:
            m_refs[h][...] = jnp.full(m_refs[h].shape, NEG_BIG, _f32)

        def step(j, masked):
            t = logits(j)
            if masked:
                t = [jnp.where(causal, th, NEG_BIG) for th in t]
            m_old = [m_refs[h][...] for h in heads]
            m_new = [jnp.maximum(m_old[h], jnp.max(t[h], axis=-1, keepdims=True)) for h in heads]
            p = [jnp.exp2(t[h] - m_new[h]).astype(_bf16) for h in heads]
            pv = [_dot(p[h], v_tile(j, h)) for h in heads]
            for h in heads:
                acc_refs[h][...] = jnp.exp2(m_old[h] - m_new[h]) * acc_refs[h][...] + pv[h]
                m_refs[h][...] = m_new[h]
        run(step)

    acc0 = acc0_ref[...]
    acc1 = acc1_ref[...]
    res0 = acc0 / acc0[:, HEAD_DIM:HEAD_DIM + 1]
    res1 = acc1 / acc1[:, 0:1]
    o_ref[0] = jnp.where(_head_masks(res0.shape), res0, res1).astype(o_ref.dtype)


def _fox_attn(q, k, v_aug, frow, fcol, fedge, tile):
    b, s, _ = q.shape
    n_pairs = D_BRANCH // LANES
    nk = s // tile
    return pl.pallas_call(
        functools.partial(_fox_kernel, tile=tile),
        grid=(b, n_pairs, nk),
        in_specs=[
            pl.BlockSpec((1, tile, LANES), lambda bi, hp, i: (bi, i, hp)),
            pl.BlockSpec((1, s, LANES), lambda bi, hp, i: (bi, 0, hp)),
            pl.BlockSpec((1, s, 2 * LANES), lambda bi, hp, i: (bi, 0, hp)),
            pl.BlockSpec((1, 1, nk, 2, tile), lambda bi, hp, i: (bi, hp, 0, 0, 0)),
            pl.BlockSpec((1, 1, tile, 2), lambda bi, hp, i: (bi, hp, i, 0)),
            pl.BlockSpec((1, 1, 4, nk), lambda bi, hp, i: (bi, hp, 0, 0), memory_space=pltpu.SMEM),
        ],
        out_specs=pl.BlockSpec((1, tile, LANES), lambda bi, hp, i: (bi, i, hp)),
        out_shape=jax.ShapeDtypeStruct((b, s, D_BRANCH), _bf16),
        scratch_shapes=[pltpu.VMEM((tile, 1), _f32), pltpu.VMEM((tile, 1), _f32),
                        pltpu.VMEM((tile, LANES), _f32), pltpu.VMEM((tile, LANES), _f32),
                        pltpu.VMEM((8, LANES), _f32)],
        compiler_params=pltpu.CompilerParams(
            dimension_semantics=("arbitrary", "arbitrary", "arbitrary"),
            vmem_limit_bytes=VMEM_LIMIT),
        name="fox_attn",
    )(q, k, v_aug, frow, fcol, fedge)


def _sb_kernel(q_ref, k_ref, v_ref, tri_ref, o_ref, c0_ref, c1_ref, acc0_ref, acc1_ref, *, tile):
    i = pl.program_id(2)
    q = q_ref[0]
    first_head = _head_masks(q.shape)
    zero = jnp.zeros_like(q)
    qh = (jnp.where(first_head, q, zero), jnp.where(first_head, zero, q))
    row = lax.broadcasted_iota(jnp.int32, (tile, tile), 0)
    colm = lax.broadcasted_iota(jnp.int32, (tile, tile), 1)
    strict = colm < row
    tri2 = tri_ref[...]
    c_refs = (c0_ref, c1_ref)
    acc_refs = (acc0_ref, acc1_ref)
    heads = range(2)
    sign_bit = jnp.uint32(0x80000000)

    def sweep(tiles, c):
        n_t = range(len(tiles))
        kt = [k_ref[0, pl.ds(pl.multiple_of(j * tile, tile), tile), :] for j, _, _ in tiles]
        vt = [v_ref[0, pl.ds(pl.multiple_of(j * tile, tile), tile), :] for j, _, _ in tiles]
        vt = [v if s is None else v * s for v, (_, _, s) in zip(vt, tiles)]
        z = [[_dot_nt(qh[h], kt[u]) for h in heads] for u in n_t]
        neg_abs = [[pltpu.bitcast(pltpu.bitcast(z[u][h], jnp.uint32) | sign_bit, _f32) for h in heads]
                   for u in n_t]
        sp = [[jnp.maximum(z[u][h], 0.0) + jnp.log2(1.0 + jnp.exp2(neg_abs[u][h])) for h in heads]
              for u in n_t]
        sp = [[jnp.where(strict, s, 0.0) for s in sp[u]] if tiles[u][1] else sp[u] for u in n_t]
        incl = [[_dot(jnp.concatenate(_split2(sp[u][h]), axis=1), tri2) for h in heads] for u in n_t]
        pv = [None, None]
        for u in n_t:
            a = [jnp.exp2(z[u][h] - incl[u][h] - c[h]) for h in heads]
            if tiles[u][1]:
                a = [jnp.where(strict, ah, 0.0) for ah in a]
            for h in heads:
                d = _dot(a[h].astype(_bf16), vt[u])
                pv[h] = d if pv[h] is None else pv[h] + d
            c = [c[h] + incl[u][h][:, 0:1] for h in heads]
        return pv, c

    has_prev = (i > 0).astype(_bf16)
    zero_c = jnp.zeros(c0_ref.shape, _f32)
    pv, c = sweep([(i, True, None), (jnp.maximum(i - 1, 0), False, has_prev)], [zero_c, zero_c])
    for h in heads:
        acc_refs[h][...] = pv[h]
        c_refs[h][...] = c[h]

    def cond(carry):
        j, c_min = carry
        return jnp.logical_and(j >= 0, c_min < SB_DEAD)

    def body(carry):
        j, _ = carry
        pv, c = sweep([(j, False, None)], [c_refs[h][...] for h in heads])
        for h in heads:
            acc_refs[h][...] += pv[h]
            c_refs[h][...] = c[h]
        return j - 1, jnp.min(jnp.minimum(c[0], c[1]))

    lax.while_loop(cond, body, (i - 2, jnp.min(jnp.minimum(c[0], c[1]))))
    o_ref[0] = jnp.where(first_head, acc0_ref[...], acc1_ref[...]).astype(o_ref.dtype)


def _sb_attn(q, k, v, tile):
    b, s, _ = q.shape
    n_pairs = D_BRANCH // LANES
    nk = s // tile
    tri = (lax.broadcasted_iota(jnp.int32, (tile, tile), 0)
           >= lax.broadcasted_iota(jnp.int32, (tile, tile), 1)).astype(_bf16)
    tri = jnp.concatenate([tri, tri], axis=0)
    return pl.pallas_call(
        functools.partial(_sb_kernel, tile=tile),
        grid=(b, n_pairs, nk),
        in_specs=[
            pl.BlockSpec((1, tile, LANES), lambda bi, hp, i: (bi, i, hp)),
            pl.BlockSpec((1, s, LANES), lambda bi, hp, i: (bi, 0, hp)),
            pl.BlockSpec((1, s, LANES), lambda bi, hp, i: (bi, 0, hp)),
            pl.BlockSpec((2 * tile, tile), lambda bi, hp, i: (0, 0)),
        ],
        out_specs=pl.BlockSpec((1, tile, LANES), lambda bi, hp, i: (bi, i, hp)),
        out_shape=jax.ShapeDtypeStruct((b, s, D_BRANCH), _bf16),
        scratch_shapes=[pltpu.VMEM((tile, 1), _f32), pltpu.VMEM((tile, 1), _f32),
                        pltpu.VMEM((tile, LANES), _f32), pltpu.VMEM((tile, LANES), _f32)],
        compiler_params=pltpu.CompilerParams(
            dimension_semantics=("arbitrary", "arbitrary", "arbitrary"),
            vmem_limit_bytes=VMEM_LIMIT),
        name="sb_attn",
    )(q, k, v, tri)


def _post_attn_kernel(x_ref, lng_ref, lnb_ref, yf_ref, ys_ref, gf_ref, gs_ref,
                      wuf_ref, wus_ref, wo_ref, g1_ref, b1_ref, wr_hi_ref, wr_lo_ref, br_ref,
                      h1_ref, route_ref):
    h0 = _layer_norm(x_ref[...], lng_ref[...], lnb_ref[...])
    up_f = _dot(yf_ref[...], wuf_ref[...])
    up_s = _dot(ys_ref[...], wus_ref[...])
    sig_f = 1.0 / (1.0 + jnp.exp(-gf_ref[...].astype(_f32)))
    sig_s = 1.0 / (1.0 + jnp.exp(-gs_ref[...].astype(_f32)))
    merged = sig_f * up_f + sig_s * up_s
    mix = _dot(merged.astype(_bf16), wo_ref[...])
    h1 = _layer_norm(DN_ALPHA * h0 + mix, g1_ref[...], b1_ref[...])
    h1_ref[...] = h1

    h_hi, h_lo = _split2(h1)
    logits = (_dot(h_hi, wr_hi_ref[...]) + _dot(h_lo, wr_hi_ref[...])
              + _dot(h_hi, wr_lo_ref[...])) + br_ref[...]
    lane = lax.broadcasted_iota(jnp.int32, logits.shape, 1)
    lane_f = lane.astype(_f32)
    big = float(LANES)

    g_log = jnp.where(lane < N_GROUPS, logits, -jnp.inf)
    g_max = jnp.max(g_log, axis=-1, keepdims=True)
    g_sel = jnp.min(jnp.where(g_log == g_max, lane_f, big), axis=-1, keepdims=True)
    g_gate = 1.0 / jnp.sum(jnp.exp(g_log - g_max), axis=-1, keepdims=True)

    lo_lane = N_GROUPS + g_sel * EXPERTS_PER_GROUP
    in_grp = jnp.logical_and(lane_f >= lo_lane, lane_f < lo_lane + EXPERTS_PER_GROUP)
    e_log = jnp.where(in_grp, logits, -jnp.inf)
    e1 = jnp.max(e_log, axis=-1, keepdims=True)
    i1 = jnp.min(jnp.where(e_log == e1, lane_f, big), axis=-1, keepdims=True)
    e_log2 = jnp.where(lane_f == i1, -jnp.inf, e_log)
    e2 = jnp.max(e_log2, axis=-1, keepdims=True)
    i2 = jnp.min(jnp.where(e_log2 == e2, lane_f, big), axis=-1, keepdims=True)
    d = jnp.exp(e2 - e1)
    w1 = 1.0 / (1.0 + d)
    w2 = d * w1

    route = jnp.where(lane == 0, g_gate * w1, 0.0)
    route = jnp.where(lane == 1, g_gate * w2, route)
    route = jnp.where(lane == 2, i1 - N_GROUPS, route)
    route = jnp.where(lane == 3, i2 - N_GROUPS, route)
    route_ref[...] = route


def _post_attn(x2, lng, lnb, yf, ys, gf, gs, wuf, wus, wo, g1, b1, wr_hi, wr_lo, br):
    n, d = x2.shape
    tm = TOKEN_TILE
    const = lambda i: (0, 0)
    row = lambda i: (i, 0)
    vec = pl.BlockSpec((1, d), const)
    return pl.pallas_call(
        _post_attn_kernel,
        grid=(n // tm,),
        in_specs=[
            pl.BlockSpec((tm, d), row), vec, vec,
            pl.BlockSpec((tm, D_BRANCH), row), pl.BlockSpec((tm, D_BRANCH), row),
            pl.BlockSpec((tm, d), row), pl.BlockSpec((tm, d), row),
            pl.BlockSpec(wuf.shape, const), pl.BlockSpec(wus.shape, const),
            pl.BlockSpec(wo.shape, const), vec, vec,
            pl.BlockSpec(wr_hi.shape, const), pl.BlockSpec(wr_lo.shape, const),
            pl.BlockSpec((1, LANES), const),
        ],
        out_specs=[pl.BlockSpec((tm, d), row), pl.BlockSpec((tm, LANES), row)],
        out_shape=[jax.ShapeDtypeStruct((n, d), _f32), jax.ShapeDtypeStruct((n, LANES), _f32)],
        compiler_params=pltpu.CompilerParams(
            dimension_semantics=("arbitrary",), vmem_limit_bytes=VMEM_LIMIT),
        name="post_attn",
    )(x2, lng, lnb, yf, ys, gf, gs, wuf, wus, wo, g1, b1, wr_hi, wr_lo, br)


def _experts_kernel(blk_expert_ref, src_ref, src_next_ref, h_hbm, w1_ref, w3_ref, w2_ref,
                    y_ref, buf_ref, sem_ref):
    del blk_expert_ref
    b = pl.program_id(0)
    nb = pl.num_programs(0)
    rows = buf_ref.shape[1]
    slot = b % 2

    def row_copy(idx_ref, r, slot):
        return pltpu.make_async_copy(h_hbm.at[pl.ds(idx_ref[0, 0, r], 1), :],
                                     buf_ref.at[slot, pl.ds(r, 1), :], sem_ref.at[slot])

    @pl.when(b == 0)
    def _():
        def body(r, carry):
            for u in range(DMA_UNROLL):
                row_copy(src_ref, r * DMA_UNROLL + u, 0).start()
            return carry
        lax.fori_loop(0, rows // DMA_UNROLL, body, 0)

    pltpu.make_async_copy(buf_ref.at[slot], buf_ref.at[slot], sem_ref.at[slot]).wait()

    def compute(prefetch):
        if prefetch:
            for r in range(rows):
                row_copy(src_next_ref, r, 1 - slot).start()
        xb = buf_ref[slot].astype(_bf16)
        a = _dot(xb, w1_ref[0])
        g = _dot(xb, w3_ref[0])
        hidden = (a * (1.0 / (1.0 + jnp.exp(-a)))) * g
        y_ref[...] = _dot(hidden.astype(_bf16), w2_ref[0])

    pl.when(b + 1 < nb)(functools.partial(compute, True))
    pl.when(b + 1 >= nb)(functools.partial(compute, False))


def _experts(blk_expert, src3, h1, w1b, w3b, w2b):
    n_blocks = blk_expert.shape[0]
    rows = DISPATCH_BLOCK
    d = h1.shape[1]
    de = w1b.shape[2]
    grid_spec = pltpu.PrefetchScalarGridSpec(
        num_scalar_prefetch=1,
        grid=(n_blocks,),
        in_specs=[
            pl.BlockSpec((1, 1, rows), lambda b, be: (b, 0, 0), memory_space=pltpu.SMEM),
            pl.BlockSpec((1, 1, rows), lambda b, be: (jnp.minimum(b + 1, n_blocks - 1), 0, 0),
                         memory_space=pltpu.SMEM),
            pl.BlockSpec(memory_space=pl.ANY),
            pl.BlockSpec((1, d, de), lambda b, be: (be[b], 0, 0)),
            pl.BlockSpec((1, d, de), lambda b, be: (be[b], 0, 0)),
            pl.BlockSpec((1, de, d), lambda b, be: (be[b], 0, 0)),
        ],
        out_specs=pl.BlockSpec((rows, d), lambda b, be: (b, 0)),
        scratch_shapes=[pltpu.VMEM((2, rows, d), _f32), pltpu.SemaphoreType.DMA((2,))],
    )
    return pl.pallas_call(
        _experts_kernel,
        grid_spec=grid_spec,
        out_shape=jax.ShapeDtypeStruct((n_blocks * rows, d), _f32),
        compiler_params=pltpu.CompilerParams(
            dimension_semantics=("arbitrary",), vmem_limit_bytes=VMEM_LIMIT),
        name="experts",
    )(blk_expert, src3, src3, h1, w1b, w3b, w2b)


def _combine_kernel(pos_ref, pos_next_ref, h1_ref, gate_ref, g2_ref, b2_ref, y_hbm, o_ref, buf_ref, sem_ref):
    tm = o_ref.shape[0]
    i = pl.program_id(0)
    n_steps = pl.num_programs(0)
    slot = i % 2

    def start_tile(idx_ref, slot):
        def body(r, carry):
            for u in range(DMA_UNROLL):
                rr = r * DMA_UNROLL + u
                pltpu.make_async_copy(y_hbm.at[pl.ds(idx_ref[0, 0, rr], 1), :],
                                      buf_ref.at[slot, pl.ds(rr, 1), :], sem_ref.at[slot]).start()
            return carry
        lax.fori_loop(0, TOP_K * tm // DMA_UNROLL, body, 0)

    @pl.when(i == 0)
    def _():
        start_tile(pos_ref, 0)

    pltpu.make_async_copy(buf_ref.at[slot], buf_ref.at[slot], sem_ref.at[slot]).wait()

    def compute(prefetch):
        if prefetch:
            for rr in range(TOP_K * tm):
                pltpu.make_async_copy(y_hbm.at[pl.ds(pos_next_ref[0, 0, rr], 1), :],
                                      buf_ref.at[1 - slot, pl.ds(rr, 1), :], sem_ref.at[1 - slot]).start()
        gate = gate_ref[...]
        ffn = gate[:, 0:1] * buf_ref[slot, 0:tm, :] + gate[:, 1:2] * buf_ref[slot, tm:2 * tm, :]
        o_ref[...] = _layer_norm(DN_ALPHA * h1_ref[...] + ffn, g2_ref[...], b2_ref[...])

    pl.when(i + 1 < n_steps)(functools.partial(compute, True))
    pl.when(i + 1 >= n_steps)(functools.partial(compute, False))


def _combine(pos3, h1, route, g2, b2, y_pad):
    n, d = h1.shape
    tm = COMBINE_TILE
    const = lambda i: (0, 0)
    row = lambda i: (i, 0)
    return pl.pallas_call(
        _combine_kernel,
        grid=(n // tm,),
        in_specs=[
            pl.BlockSpec((1, 1, TOP_K * tm), lambda i: (i, 0, 0), memory_space=pltpu.SMEM),
            pl.BlockSpec((1, 1, TOP_K * tm), lambda i: (jnp.minimum(i + 1, n // tm - 1), 0, 0),
                         memory_space=pltpu.SMEM),
            pl.BlockSpec((tm, d), row),
            pl.BlockSpec((tm, LANES), row),
            pl.BlockSpec((1, d), const), pl.BlockSpec((1, d), const),
            pl.BlockSpec(memory_space=pl.ANY),
        ],
        out_specs=pl.BlockSpec((tm, d), row),
        out_shape=jax.ShapeDtypeStruct((n, d), _f32),
        scratch_shapes=[pltpu.VMEM((2, TOP_K * tm, d), _f32), pltpu.SemaphoreType.DMA((2,))],
        compiler_params=pltpu.CompilerParams(
            dimension_semantics=("arbitrary",), vmem_limit_bytes=VMEM_LIMIT),
        name="combine",
    )(pos3, pos3, h1, route, g2, b2, y_pad)


def _pad_lanes(w):
    return jnp.pad(w, ((0, 0), (0, LANES - w.shape[1])))


def _dispatch_plan(expert_id):
    n = expert_id.shape[0]
    m = n * TOP_K
    flat = expert_id.reshape(m)
    onehot = (flat[:, None] == jnp.arange(N_EXPERTS, dtype=jnp.int32)[None, :]).astype(jnp.int32)
    incl = jnp.cumsum(onehot, axis=0)
    counts = incl[-1]
    rank = jnp.sum((incl - onehot) * onehot, axis=1)
    blocks_per = (counts + DISPATCH_BLOCK - 1) // DISPATCH_BLOCK
    blk_end = jnp.cumsum(blocks_per)
    pstarts = (blk_end - blocks_per) * DISPATCH_BLOCK
    dest = pstarts[flat] + rank
    n_blocks = (m + N_EXPERTS * (DISPATCH_BLOCK - 1) + DISPATCH_BLOCK - 1) // DISPATCH_BLOCK
    n_used = blk_end[-1]
    src = jnp.zeros((n_blocks * DISPATCH_BLOCK,), jnp.int32).at[dest].set(
        jnp.arange(m, dtype=jnp.int32) // TOP_K)
    blk = jnp.arange(n_blocks, dtype=jnp.int32)
    blk_expert = jnp.sum((blk[:, None] >= blk_end[None, :]).astype(jnp.int32), axis=1)
    last_expert = jnp.sum((jnp.maximum(n_used - 1, 0) >= blk_end).astype(jnp.int32))
    blk_expert = jnp.where(blk < n_used, blk_expert, last_expert).astype(jnp.int32)
    return dest.reshape(n, TOP_K), src.reshape(n_blocks, 1, DISPATCH_BLOCK), blk_expert


def kernel(x, ln_in_g, ln_in_b, w_in, b_forget, w_up_fox, w_up_sb, w_out, ln1_g, ln1_b, w_group, b_group,
           w_expert_router, b_expert_router, w1, w3, w2, ln2_g, ln2_b):
    bsz, seq, d = x.shape
    n = bsz * seq
    assert w_in.shape[0] == DEPTH
    x2 = x.reshape(n, d)
    scale = HEAD_DIM ** -0.5

    wi = w_in[0]
    c = D_BRANCH
    off_f = 3 * c
    off_s = off_f + N_HEADS
    off_g = off_s + 3 * c
    perm = jnp.argsort(b_forget[0])
    cols = (perm[:, None] * HEAD_DIM + jnp.arange(HEAD_DIM)[None, :]).reshape(c)
    w_main = jnp.concatenate([
        wi[:, 0:c][:, cols] * (scale * LOG2E), wi[:, c:2 * c][:, cols], wi[:, 2 * c:3 * c][:, cols],
        wi[:, off_s:off_s + c] * (scale * LOG2E), wi[:, off_s + c:off_g],
        wi[:, off_g:],
    ], axis=1).astype(_bf16)
    wf = _pad_lanes(wi[:, off_f:off_s][:, perm])
    wf_hi, wf_lo = _split2(wf)
    bf_pad = _pad_lanes(b_forget[0][perm][None, :])
    w_up_fox_p = w_up_fox[0][cols, :]

    row = lambda v: v.reshape(1, -1)
    qf, kf, vf, qs, ks, vs, gf, gs, fcum = _inproj(
        x2, row(ln_in_g), row(ln_in_b), w_main, wf_hi, wf_lo, bf_pad, seq)

    to3 = lambda t: t.reshape(bsz, seq, D_BRANCH)
    nk = seq // FOX_TILE
    f5 = fcum[:, :N_HEADS].reshape(bsz, nk, FOX_TILE, N_HEADS // 2, 2)
    frow = f5.transpose(0, 3, 1, 4, 2)
    fcol = f5.reshape(bsz, seq, N_HEADS // 2, 2).transpose(0, 2, 1, 3)
    fedge = jnp.concatenate([f5[:, :, 0], f5[:, :, FOX_TILE - 1]], axis=-1)
    fedge = fedge.transpose(0, 2, 3, 1)
    vf_aug = vf.reshape(bsz, seq, 2 * D_BRANCH)
    y_fox = _fox_attn(to3(qf), to3(kf), vf_aug, frow, fcol, fedge, FOX_TILE).reshape(n, D_BRANCH)
    y_sb = _sb_attn(to3(qs), to3(ks), to3(vs), SB_TILE).reshape(n, D_BRANCH)

    wr = _pad_lanes(jnp.concatenate([w_group[0], w_expert_router[0]], axis=1))
    wr_hi, wr_lo = _split2(wr)
    br = _pad_lanes(jnp.concatenate([b_group[0], b_expert_router[0]])[None, :])
    h1, route = _post_attn(
        x2, row(ln_in_g), row(ln_in_b), y_fox, y_sb, gf, gs,
        w_up_fox_p.astype(_bf16), w_up_sb[0].astype(_bf16), w_out[0].astype(_bf16),
        row(ln1_g[0]), row(ln1_b[0]), wr_hi, wr_lo, br)

    expert_id = route[:, 2:4].astype(jnp.int32)
    dest, src3, blk_expert = _dispatch_plan(expert_id)
    y_pad = _experts(blk_expert, src3, h1,
                     w1[0].astype(_bf16), w3[0].astype(_bf16), w2[0].astype(_bf16))

    tm = COMBINE_TILE
    pos3 = dest.reshape(n // tm, tm, TOP_K).transpose(0, 2, 1).reshape(n // tm, 1, TOP_K * tm)
    out = _combine(pos3, h1, route, row(ln2_g[0]), row(ln2_b[0]), y_pad)
    return out.reshape(bsz, seq, d)
```

```python
import functools

import jax
import jax.numpy as jnp
from jax import lax
from jax.experimental import pallas as pl
from jax.experimental.pallas import tpu as pltpu

HEAD_DIM = 64
N_HEADS = 8
D_BRANCH = N_HEADS * HEAD_DIM
N_GROUPS = 4
EXPERTS_PER_GROUP = 8
N_EXPERTS = N_GROUPS * EXPERTS_PER_GROUP
TOP_K = 2
DISPATCH_BLOCK = 256
DEPTH = 1
DN_ALPHA = (2.0 * DEPTH) ** 0.25
LN_EPS = 1e-5

LANES = 128
TOKEN_TILE = 512
FOX_TILE = 512
FOX_SAFE_DOT = 60.0
FOX_DEAD = 150.0
SB_TILE = 256
SB_DEAD = 150.0
LOG2E = 1.4426950408889634
COMBINE_TILE = 256
DMA_UNROLL = 8
VMEM_LIMIT = 56 * 1024 * 1024
NEG_BIG = -1e30

_f32 = jnp.float32
_bf16 = jnp.bfloat16


def _layer_norm(x, g, b):
    mu = jnp.mean(x, axis=-1, keepdims=True)
    xc = x - mu
    var = jnp.mean(xc * xc, axis=-1, keepdims=True)
    return xc * lax.rsqrt(var + LN_EPS) * g + b


def _split2(x):
    hi = x.astype(_bf16)
    lo = (x - hi.astype(_f32)).astype(_bf16)
    return hi, lo


def _split3(x):
    hi = x.astype(_bf16)
    r = x - hi.astype(_f32)
    mid = r.astype(_bf16)
    lo = (r - mid.astype(_f32)).astype(_bf16)
    return hi, mid, lo


def _dot(a, b):
    return jnp.dot(a, b, preferred_element_type=_f32)


def _dot_nt(a, b):
    return lax.dot_general(a, b, (((1,), (1,)), ((), ())), preferred_element_type=_f32)


def _inproj_kernel(x_ref, g_ref, b_ref, w_ref, wf_hi_ref, wf_lo_ref, bf_ref, tri_ref,
                   qf_ref, kf_ref, vf_ref, qs_ref, ks_ref, vs_ref, gf_ref, gs_ref, fcum_ref,
                   carry_ref, *, tiles_per_seq):
    i = pl.program_id(0)
    h = _layer_norm(x_ref[...], g_ref[...], b_ref[...])
    h_hi, h_lo = _split2(h)

    col = 0
    for ref in (qf_ref, kf_ref, None, qs_ref, ks_ref, vs_ref, gf_ref, gs_ref):
        width = D_BRANCH if ref is None else ref.shape[-1]
        res = _dot(h_hi, w_ref[:, col:col + width])
        col += width
        if ref is not None:
            ref[...] = res.astype(ref.dtype)
            continue
        lane = lax.broadcasted_iota(jnp.int32, (res.shape[0], LANES), 1)
        one_at = lambda k: jnp.where(lane == k, 1.0, 0.0)
        for p in range(D_BRANCH // LANES):
            vp = res[:, p * LANES:(p + 1) * LANES]
            first = jnp.where(lane < HEAD_DIM, vp, one_at(HEAD_DIM))
            second = jnp.where(lane < HEAD_DIM, one_at(0), vp)
            vf_ref[:, 2 * p * LANES:(2 * p + 1) * LANES] = first.astype(vf_ref.dtype)
            vf_ref[:, (2 * p + 1) * LANES:(2 * p + 2) * LANES] = second.astype(vf_ref.dtype)

    f_logit = (_dot(h_hi, wf_hi_ref[...]) + _dot(h_lo, wf_hi_ref[...])
               + _dot(h_hi, wf_lo_ref[...])) + bf_ref[...]
    log_f = jnp.minimum(f_logit, 0.0) - jnp.log(1.0 + jnp.exp(-jnp.abs(f_logit)))

    p0, p1, p2 = _split3(log_f)
    tri = tri_ref[...]
    cum = _dot(tri, p0) + _dot(tri, p1) + _dot(tri, p2)

    @pl.when(i % tiles_per_seq == 0)
    def _():
        carry_ref[...] = jnp.zeros_like(carry_ref)

    cum = cum + carry_ref[...]
    fcum_ref[...] = cum * LOG2E
    carry_ref[...] = cum[-1:, :]


def _inproj(x2, ln_g, ln_b, w_main, wf_hi, wf_lo, bf_pad, seq_len):
    n, d = x2.shape
    tm = TOKEN_TILE
    assert n % tm == 0 and seq_len % tm == 0
    tri = (lax.broadcasted_iota(jnp.int32, (tm, tm), 1)
           <= lax.broadcasted_iota(jnp.int32, (tm, tm), 0)).astype(_bf16)
    const = lambda i: (0, 0)
    row = lambda i: (i, 0)
    widths = (D_BRANCH, D_BRANCH, 2 * D_BRANCH) + (D_BRANCH,) * 3 + (d, d)
    out_shape = [jax.ShapeDtypeStruct((n, w), _bf16) for w in widths]
    out_shape.append(jax.ShapeDtypeStruct((n, LANES), _f32))
    out_specs = [pl.BlockSpec((tm, w), row) for w in widths] + [pl.BlockSpec((tm, LANES), row)]
    return pl.pallas_call(
        functools.partial(_inproj_kernel, tiles_per_seq=seq_len // tm),
        grid=(n // tm,),
        in_specs=[
            pl.BlockSpec((tm, d), row),
            pl.BlockSpec((1, d), const),
            pl.BlockSpec((1, d), const),
            pl.BlockSpec(w_main.shape, const),
            pl.BlockSpec(wf_hi.shape, const),
            pl.BlockSpec(wf_lo.shape, const),
            pl.BlockSpec((1, LANES), const),
            pl.BlockSpec((tm, tm), const),
        ],
        out_specs=out_specs,
        out_shape=out_shape,
        scratch_shapes=[pltpu.VMEM((1, LANES), _f32)],
        compiler_params=pltpu.CompilerParams(
            dimension_semantics=("arbitrary",), vmem_limit_bytes=VMEM_LIMIT),
        name="inproj",
    )(x2, ln_g, ln_b, w_main, wf_hi, wf_lo, bf_pad, tri)


def _head_masks(shape):
    lane = lax.broadcasted_iota(jnp.int32, shape, 1)
    return lane < HEAD_DIM


def _fox_kernel(q_ref, k_ref, v_ref, frow_ref, fcol_ref, fedge_ref, o_ref, m0_ref, m1_ref, acc0_ref, acc1_ref,
                kmax_ref, *, tile):
    i = pl.program_id(2)
    nk = pl.num_programs(2)
    heads = range(2)
    q = q_ref[0]
    first_head = _head_masks(q.shape)
    zero_q = jnp.zeros_like(q)
    qh = (jnp.where(first_head, q, zero_q), jnp.where(first_head, zero_q, q))
    fcol = [fcol_ref[0, 0, :, h:h + 1] for h in heads]
    row = lax.broadcasted_iota(jnp.int32, (tile, tile), 0)
    colm = lax.broadcasted_iota(jnp.int32, (tile, tile), 1)
    causal = colm <= row
    m_refs = (m0_ref, m1_ref)
    acc_refs = (acc0_ref, acc1_ref)
    in_first = lax.broadcasted_iota(jnp.int32, (LANES, LANES), 0) < HEAD_DIM
    head_sum = (in_first.astype(_bf16), jnp.logical_not(in_first).astype(_bf16))

    def max_sq_norms(x):
        xf = x.astype(_f32)
        sq = (xf * xf).astype(_bf16)
        return [jnp.max(_dot(sq, head_sum[h]), axis=0, keepdims=True) for h in heads]

    def k_tile(j):
        return k_ref[0, pl.ds(pl.multiple_of(j * tile, tile), tile), :]

    def v_tile(j, h):
        return v_ref[0, pl.ds(pl.multiple_of(j * tile, tile), tile), h * LANES:(h + 1) * LANES]

    @pl.when(i == 0)
    def _():
        def body(c, carry):
            n2 = max_sq_norms(k_tile(c))
            return tuple(jnp.maximum(carry[h], n2[h]) for h in heads)
        zero = jnp.zeros((1, LANES), _f32)
        km = lax.fori_loop(0, nk, body, (zero, zero))
        for h in heads:
            kmax_ref[h:h + 1, :] = km[h]

    for h in heads:
        acc_refs[h][...] = jnp.zeros(acc_refs[h].shape, _f32)

    qn2 = max_sq_norms(q)
    bound_sq = [jnp.max(qn2[h] * kmax_ref[h:h + 1, :]) * 1.05 for h in heads]
    safe = jnp.maximum(bound_sq[0], bound_sq[1]) <= FOX_SAFE_DOT * FOX_SAFE_DOT

    def logits(j):
        kt = k_tile(j)
        return [(_dot_nt(qh[h], kt) - frow_ref[0, 0, j, h:h + 1, :]) + fcol[h] for h in heads]

    @pl.when(safe)
    def _():
        def step(j, masked):
            p = [jnp.exp2(t) for t in logits(j)]
            if masked:
                p = [jnp.where(causal, ph, 0.0) for ph in p]
            pv = [_dot(p[h].astype(_bf16), v_tile(j, h)) for h in heads]
            for h in heads:
                acc_refs[h][...] += pv[h]

        def alive(j):
            res = None
            for h in heads:
                x = fedge_ref[0, 0, 2 + h, jnp.maximum(j, 0)] - fedge_ref[0, 0, h, i] - FOX_DEAD
                a = jnp.logical_or(x <= 0.0, bound_sq[h] >= x * x)
                res = a if res is None else jnp.logical_or(res, a)
            return res

        step(i, True)

        def body(j):
            step(j, False)
            return j - 1

        lax.while_loop(lambda j: jnp.logical_and(j >= 0, alive(j)), body, i - 1)

    def run(step):
        def body(j, carry):
            step(j, False)
            return carry
        lax.fori_loop(0, i, body, 0)
        step(i, True)

    @pl.when(jnp.logical_not(safe))
    def _():
        for h in heads:
            m_refs[h][...] = jnp.full(m_refs[h].shape, NEG_BIG, _f32)

        def step(j, masked):
            t = logits(j)
            if masked:
                t = [jnp.where(causal, th, NEG_BIG) for th in t]
            m_old = [m_refs[h][...] for h in heads]
            m_new = [jnp.maximum(m_old[h], jnp.max(t[h], axis=-1, keepdims=True)) for h in heads]
            p = [jnp.exp2(t[h] - m_new[h]).astype(_bf16) for h in heads]
            pv = [_dot(p[h], v_tile(j, h)) for h in heads]
            for h in heads:
                acc_refs[h][...] = jnp.exp2(m_old[h] - m_new[h]) * acc_refs[h][...] + pv[h]
                m_refs[h][...] = m_new[h]
        run(step)

    acc0 = acc0_ref[...]
    acc1 = acc1_ref[...]
    res0 = acc0 / acc0[:, HEAD_DIM:HEAD_DIM + 1]
    res1 = acc1 / acc1[:, 0:1]
    o_ref[0] = jnp.where(_head_masks(res0.shape), res0, res1).astype(o_ref.dtype)


def _fox_attn(q, k, v_aug, frow, fcol, fedge, tile):
    b, s, _ = q.shape
    n_pairs = D_BRANCH // LANES
    nk = s // tile
    return pl.pallas_call(
        functools.partial(_fox_kernel, tile=tile),
        grid=(b, n_pairs, nk),
        in_specs=[
            pl.BlockSpec((1, tile, LANES), lambda bi, hp, i: (bi, i, hp)),
            pl.BlockSpec((1, s, LANES), lambda bi, hp, i: (bi, 0, hp)),
            pl.BlockSpec((1, s, 2 * LANES), lambda bi, hp, i: (bi, 0, hp)),
            pl.BlockSpec((1, 1, nk, 2, tile), lambda bi, hp, i: (bi, hp, 0, 0, 0)),
            pl.BlockSpec((1, 1, tile, 2), lambda bi, hp, i: (bi, hp, i, 0)),
            pl.BlockSpec((1, 1, 4, nk), lambda bi, hp, i: (bi, hp, 0, 0), memory_space=pltpu.SMEM),
        ],
        out_specs=pl.BlockSpec((1, tile, LANES), lambda bi, hp, i: (bi, i, hp)),
        out_shape=jax.ShapeDtypeStruct((b, s, D_BRANCH), _bf16),
        scratch_shapes=[pltpu.VMEM((tile, 1), _f32), pltpu.VMEM((tile, 1), _f32),
                        pltpu.VMEM((tile, LANES), _f32), pltpu.VMEM((tile, LANES), _f32),
                        pltpu.VMEM((8, LANES), _f32)],
        compiler_params=pltpu.CompilerParams(
            dimension_semantics=("arbitrary", "arbitrary", "arbitrary"),
            vmem_limit_bytes=VMEM_LIMIT),
        name="fox_attn",
    )(q, k, v_aug, frow, fcol, fedge)


def _sb_kernel(q_ref, k_ref, v_ref, tri_ref, o_ref, *scratch, tile):
    i = pl.program_id(2)
    heads = range(2)
    subs = range(2)
    c_refs = [[scratch[2 * s + h] for h in heads] for s in subs]
    acc_refs = [[scratch[4 + 2 * s + h] for h in heads] for s in subs]
    first_head = _head_masks((tile, LANES))
    qh = []
    for s in subs:
        q = q_ref[0, s * tile:(s + 1) * tile, :]
        zero = jnp.zeros_like(q)
        qh.append((jnp.where(first_head, q, zero), jnp.where(first_head, zero, q)))
    row = lax.broadcasted_iota(jnp.int32, (tile, tile), 0)
    colm = lax.broadcasted_iota(jnp.int32, (tile, tile), 1)
    strict = colm < row
    tri2 = tri_ref[...]
    sign_bit = jnp.uint32(0x80000000)

    def sweep(tiles, c):
        n_t = range(len(tiles))
        kt = [k_ref[0, pl.ds(pl.multiple_of(t[0] * tile, tile), tile), :] for t in tiles]
        vt = [v_ref[0, pl.ds(pl.multiple_of(t[0] * tile, tile), tile), :] for t in tiles]
        vt = [v if t[2] is None else v * t[2] for v, t in zip(vt, tiles)]
        z = [[_dot_nt(qh[tiles[u][3]][h], kt[u]) for h in heads] for u in n_t]
        neg_abs = [[pltpu.bitcast(pltpu.bitcast(z[u][h], jnp.uint32) | sign_bit, _f32) for h in heads]
                   for u in n_t]
        sp = [[jnp.maximum(z[u][h], 0.0) + jnp.log2(1.0 + jnp.exp2(neg_abs[u][h])) for h in heads]
              for u in n_t]
        sp = [[jnp.where(strict, s, 0.0) for s in sp[u]] if tiles[u][1] else sp[u] for u in n_t]
        incl = [[_dot(jnp.concatenate(_split2(sp[u][h]), axis=1), tri2) for h in heads] for u in n_t]
        c = [list(cs) for cs in c]
        pv = [[None, None] for _ in subs]
        for u in n_t:
            sub = tiles[u][3]
            a = [jnp.exp2(z[u][h] - incl[u][h] - c[sub][h]) for h in heads]
            if tiles[u][1]:
                a = [jnp.where(strict, ah, 0.0) for ah in a]
            for h in heads:
                d = _dot(a[h].astype(_bf16), vt[u])
                pv[sub][h] = d if pv[sub][h] is None else pv[sub][h] + d
                c[sub][h] = c[sub][h] + incl[u][h][:, 0:1]
        return pv, c

    has_prev = (i > 0).astype(_bf16)
    zero_c = jnp.zeros(c_refs[0][0].shape, _f32)
    first = 2 * i
    pv, c = sweep([(first, True, None, 0), (first + 1, True, None, 1),
                   (jnp.maximum(first - 1, 0), False, has_prev, 0), (first, False, None, 1)],
                  [[zero_c, zero_c] for _ in subs])
    for sub in subs:
        for h in heads:
            acc_refs[sub][h][...] = pv[sub][h]
            c_refs[sub][h][...] = c[sub][h]

    for sub in subs:
        def c_min(sub=sub):
            return jnp.min(jnp.minimum(c_refs[sub][0][...], c_refs[sub][1][...]))

        def cond(carry):
            j, cm = carry
            return jnp.logical_and(j >= 0, cm < SB_DEAD)

        def body(carry, sub=sub, c_min=c_min):
            j, _ = carry
            one = [[c_refs[s][h][...] for h in heads] for s in subs]
            pv, c = sweep([(j, False, None, sub)], one)
            for h in heads:
                acc_refs[sub][h][...] += pv[sub][h]
                c_refs[sub][h][...] = c[sub][h]
            return j - 1, c_min()

        lax.while_loop(cond, body, (first + sub - 2, c_min()))
        o_ref[0, sub * tile:(sub + 1) * tile, :] = jnp.where(
            first_head, acc_refs[sub][0][...], acc_refs[sub][1][...]).astype(o_ref.dtype)


def _sb_attn(q, k, v, tile):
    b, s, _ = q.shape
    n_pairs = D_BRANCH // LANES
    assert s % (2 * tile) == 0
    nk = s // (2 * tile)
    tri = (lax.broadcasted_iota(jnp.int32, (tile, tile), 0)
           >= lax.broadcasted_iota(jnp.int32, (tile, tile), 1)).astype(_bf16)
    tri = jnp.concatenate([tri, tri], axis=0)
    return pl.pallas_call(
        functools.partial(_sb_kernel, tile=tile),
        grid=(b, n_pairs, nk),
        in_specs=[
            pl.BlockSpec((1, 2 * tile, LANES), lambda bi, hp, i: (bi, i, hp)),
            pl.BlockSpec((1, s, LANES), lambda bi, hp, i: (bi, 0, hp)),
            pl.BlockSpec((1, s, LANES), lambda bi, hp, i: (bi, 0, hp)),
            pl.BlockSpec((2 * tile, tile), lambda bi, hp, i: (0, 0)),
        ],
        out_specs=pl.BlockSpec((1, 2 * tile, LANES), lambda bi, hp, i: (bi, i, hp)),
        out_shape=jax.ShapeDtypeStruct((b, s, D_BRANCH), _bf16),
        scratch_shapes=[pltpu.VMEM((tile, 1), _f32)] * 4 + [pltpu.VMEM((tile, LANES), _f32)] * 4,
        compiler_params=pltpu.CompilerParams(
            dimension_semantics=("arbitrary", "arbitrary", "arbitrary"),
            vmem_limit_bytes=VMEM_LIMIT),
        name="sb_attn",
    )(q, k, v, tri)


def _post_attn_kernel(x_ref, lng_ref, lnb_ref, yf_ref, ys_ref, gf_ref, gs_ref,
                      wuf_ref, wus_ref, wo_ref, g1_ref, b1_ref, wr_hi_ref, wr_lo_ref, br_ref,
                      h1_ref, route_ref):
    h0 = _layer_norm(x_ref[...], lng_ref[...], lnb_ref[...])
    up_f = _dot(yf_ref[...], wuf_ref[...])
    up_s = _dot(ys_ref[...], wus_ref[...])
    sig_f = 1.0 / (1.0 + jnp.exp(-gf_ref[...].astype(_f32)))
    sig_s = 1.0 / (1.0 + jnp.exp(-gs_ref[...].astype(_f32)))
    merged = sig_f * up_f + sig_s * up_s
    mix = _dot(merged.astype(_bf16), wo_ref[...])
    h1 = _layer_norm(DN_ALPHA * h0 + mix, g1_ref[...], b1_ref[...])
    h1_ref[...] = h1

    h_hi, h_lo = _split2(h1)
    logits = (_dot(h_hi, wr_hi_ref[...]) + _dot(h_lo, wr_hi_ref[...])
              + _dot(h_hi, wr_lo_ref[...])) + br_ref[...]
    lane = lax.broadcasted_iota(jnp.int32, logits.shape, 1)
    lane_f = lane.astype(_f32)
    big = float(LANES)

    g_log = jnp.where(lane < N_GROUPS, logits, -jnp.inf)
    g_max = jnp.max(g_log, axis=-1, keepdims=True)
    g_sel = jnp.min(jnp.where(g_log == g_max, lane_f, big), axis=-1, keepdims=True)
    g_gate = 1.0 / jnp.sum(jnp.exp(g_log - g_max), axis=-1, keepdims=True)

    lo_lane = N_GROUPS + g_sel * EXPERTS_PER_GROUP
    in_grp = jnp.logical_and(lane_f >= lo_lane, lane_f < lo_lane + EXPERTS_PER_GROUP)
    e_log = jnp.where(in_grp, logits, -jnp.inf)
    e1 = jnp.max(e_log, axis=-1, keepdims=True)
    i1 = jnp.min(jnp.where(e_log == e1, lane_f, big), axis=-1, keepdims=True)
    e_log2 = jnp.where(lane_f == i1, -jnp.inf, e_log)
    e2 = jnp.max(e_log2, axis=-1, keepdims=True)
    i2 = jnp.min(jnp.where(e_log2 == e2, lane_f, big), axis=-1, keepdims=True)
    d = jnp.exp(e2 - e1)
    w1 = 1.0 / (1.0 + d)
    w2 = d * w1

    route = jnp.where(lane == 0, g_gate * w1, 0.0)
    route = jnp.where(lane == 1, g_gate * w2, route)
    route = jnp.where(lane == 2, i1 - N_GROUPS, route)
    route = jnp.where(lane == 3, i2 - N_GROUPS, route)
    route_ref[...] = route


def _post_attn(x2, lng, lnb, yf, ys, gf, gs, wuf, wus, wo, g1, b1, wr_hi, wr_lo, br):
    n, d = x2.shape
    tm = TOKEN_TILE
    const = lambda i: (0, 0)
    row = lambda i: (i, 0)
    vec = pl.BlockSpec((1, d), const)
    return pl.pallas_call(
        _post_attn_kernel,
        grid=(n // tm,),
        in_specs=[
            pl.BlockSpec((tm, d), row), vec, vec,
            pl.BlockSpec((tm, D_BRANCH), row), pl.BlockSpec((tm, D_BRANCH), row),
            pl.BlockSpec((tm, d), row), pl.BlockSpec((tm, d), row),
            pl.BlockSpec(wuf.shape, const), pl.BlockSpec(wus.shape, const),
            pl.BlockSpec(wo.shape, const), vec, vec,
            pl.BlockSpec(wr_hi.shape, const), pl.BlockSpec(wr_lo.shape, const),
            pl.BlockSpec((1, LANES), const),
        ],
        out_specs=[pl.BlockSpec((tm, d), row), pl.BlockSpec((tm, LANES), row)],
        out_shape=[jax.ShapeDtypeStruct((n, d), _f32), jax.ShapeDtypeStruct((n, LANES), _f32)],
        compiler_params=pltpu.CompilerParams(
            dimension_semantics=("arbitrary",), vmem_limit_bytes=VMEM_LIMIT),
        name="post_attn",
    )(x2, lng, lnb, yf, ys, gf, gs, wuf, wus, wo, g1, b1, wr_hi, wr_lo, br)


def _experts_kernel(blk_expert_ref, src_ref, src_next_ref, h_hbm, w1_ref, w3_ref, w2_ref,
                    y_ref, buf_ref, sem_ref):
    del blk_expert_ref
    b = pl.program_id(0)
    nb = pl.num_programs(0)
    rows = buf_ref.shape[1]
    slot = b % 2

    def row_copy(idx_ref, r, slot):
        return pltpu.make_async_copy(h_hbm.at[pl.ds(idx_ref[0, 0, r], 1), :],
                                     buf_ref.at[slot, pl.ds(r, 1), :], sem_ref.at[slot])

    @pl.when(b == 0)
    def _():
        def body(r, carry):
            for u in range(DMA_UNROLL):
                row_copy(src_ref, r * DMA_UNROLL + u, 0).start()
            return carry
        lax.fori_loop(0, rows // DMA_UNROLL, body, 0)

    pltpu.make_async_copy(buf_ref.at[slot], buf_ref.at[slot], sem_ref.at[slot]).wait()

    def compute(prefetch):
        if prefetch:
            for r in range(rows):
                row_copy(src_next_ref, r, 1 - slot).start()
        xb = buf_ref[slot].astype(_bf16)
        a = _dot(xb, w1_ref[0])
        g = _dot(xb, w3_ref[0])
        hidden = (a * (1.0 / (1.0 + jnp.exp(-a)))) * g
        y_ref[...] = _dot(hidden.astype(_bf16), w2_ref[0])

    pl.when(b + 1 < nb)(functools.partial(compute, True))
    pl.when(b + 1 >= nb)(functools.partial(compute, False))


def _experts(blk_expert, src3, h1, w1b, w3b, w2b):
    n_blocks = blk_expert.shape[0]
    rows = DISPATCH_BLOCK
    d = h1.shape[1]
    de = w1b.shape[2]
    grid_spec = pltpu.PrefetchScalarGridSpec(
        num_scalar_prefetch=1,
        grid=(n_blocks,),
        in_specs=[
            pl.BlockSpec((1, 1, rows), lambda b, be: (b, 0, 0), memory_space=pltpu.SMEM),
            pl.BlockSpec((1, 1, rows), lambda b, be: (jnp.minimum(b + 1, n_blocks - 1), 0, 0),
                         memory_space=pltpu.SMEM),
            pl.BlockSpec(memory_space=pl.ANY),
            pl.BlockSpec((1, d, de), lambda b, be: (be[b], 0, 0)),
            pl.BlockSpec((1, d, de), lambda b, be: (be[b], 0, 0)),
            pl.BlockSpec((1, de, d), lambda b, be: (be[b], 0, 0)),
        ],
        out_specs=pl.BlockSpec((rows, d), lambda b, be: (b, 0)),
        scratch_shapes=[pltpu.VMEM((2, rows, d), _f32), pltpu.SemaphoreType.DMA((2,))],
    )
    return pl.pallas_call(
        _experts_kernel,
        grid_spec=grid_spec,
        out_shape=jax.ShapeDtypeStruct((n_blocks * rows, d), _f32),
        compiler_params=pltpu.CompilerParams(
            dimension_semantics=("arbitrary",), vmem_limit_bytes=VMEM_LIMIT),
        name="experts",
    )(blk_expert, src3, src3, h1, w1b, w3b, w2b)


def _combine_kernel(pos_ref, pos_next_ref, h1_ref, gate_ref, g2_ref, b2_ref, y_hbm, o_ref, buf_ref, sem_ref):
    tm = o_ref.shape[0]
    i = pl.program_id(0)
    n_steps = pl.num_programs(0)
    slot = i % 2

    def start_tile(idx_ref, slot):
        def body(r, carry):
            for u in range(DMA_UNROLL):
                rr = r * DMA_UNROLL + u
                pltpu.make_async_copy(y_hbm.at[pl.ds(idx_ref[0, 0, rr], 1), :],
                                      buf_ref.at[slot, pl.ds(rr, 1), :], sem_ref.at[slot]).start()
            return carry
        lax.fori_loop(0, TOP_K * tm // DMA_UNROLL, body, 0)

    @pl.when(i == 0)
    def _():
        start_tile(pos_ref, 0)

    pltpu.make_async_copy(buf_ref.at[slot], buf_ref.at[slot], sem_ref.at[slot]).wait()

    def compute(prefetch):
        if prefetch:
            for rr in range(TOP_K * tm):
                pltpu.make_async_copy(y_hbm.at[pl.ds(pos_next_ref[0, 0, rr], 1), :],
                                      buf_ref.at[1 - slot, pl.ds(rr, 1), :], sem_ref.at[1 - slot]).start()
        gate = gate_ref[...]
        ffn = gate[:, 0:1] * buf_ref[slot, 0:tm, :] + gate[:, 1:2] * buf_ref[slot, tm:2 * tm, :]
        o_ref[...] = _layer_norm(DN_ALPHA * h1_ref[...] + ffn, g2_ref[...], b2_ref[...])

    pl.when(i + 1 < n_steps)(functools.partial(compute, True))
    pl.when(i + 1 >= n_steps)(functools.partial(compute, False))


def _combine(pos3, h1, route, g2, b2, y_pad):
    n, d = h1.shape
    tm = COMBINE_TILE
    const = lambda i: (0, 0)
    row = lambda i: (i, 0)
    return pl.pallas_call(
        _combine_kernel,
        grid=(n // tm,),
        in_specs=[
            pl.BlockSpec((1, 1, TOP_K * tm), lambda i: (i, 0, 0), memory_space=pltpu.SMEM),
            pl.BlockSpec((1, 1, TOP_K * tm), lambda i: (jnp.minimum(i + 1, n // tm - 1), 0, 0),
                         memory_space=pltpu.SMEM),
            pl.BlockSpec((tm, d), row),
            pl.BlockSpec((tm, LANES), row),
            pl.BlockSpec((1, d), const), pl.BlockSpec((1, d), const),
            pl.BlockSpec(memory_space=pl.ANY),
        ],
        out_specs=pl.BlockSpec((tm, d), row),
        out_shape=jax.ShapeDtypeStruct((n, d), _f32),
        scratch_shapes=[pltpu.VMEM((2, TOP_K * tm, d), _f32), pltpu.SemaphoreType.DMA((2,))],
        compiler_params=pltpu.CompilerParams(
            dimension_semantics=("arbitrary",), vmem_limit_bytes=VMEM_LIMIT),
        name="combine",
    )(pos3, pos3, h1, route, g2, b2, y_pad)


def _pad_lanes(w):
    return jnp.pad(w, ((0, 0), (0, LANES - w.shape[1])))


def _dispatch_plan(expert_id):
    n = expert_id.shape[0]
    m = n * TOP_K
    flat = expert_id.reshape(m)
    onehot = (flat[:, None] == jnp.arange(N_EXPERTS, dtype=jnp.int32)[None, :]).astype(jnp.int32)
    incl = jnp.cumsum(onehot, axis=0)
    counts = incl[-1]
    rank = jnp.sum((incl - onehot) * onehot, axis=1)
    blocks_per = (counts + DISPATCH_BLOCK - 1) // DISPATCH_BLOCK
    blk_end = jnp.cumsum(blocks_per)
    pstarts = (blk_end - blocks_per) * DISPATCH_BLOCK
    dest = pstarts[flat] + rank
    n_blocks = (m + N_EXPERTS * (DISPATCH_BLOCK - 1) + DISPATCH_BLOCK - 1) // DISPATCH_BLOCK
    n_used = blk_end[-1]
    src = jnp.zeros((n_blocks * DISPATCH_BLOCK,), jnp.int32).at[dest].set(
        jnp.arange(m, dtype=jnp.int32) // TOP_K)
    blk = jnp.arange(n_blocks, dtype=jnp.int32)
    blk_expert = jnp.sum((blk[:, None] >= blk_end[None, :]).astype(jnp.int32), axis=1)
    last_expert = jnp.sum((jnp.maximum(n_used - 1, 0) >= blk_end).astype(jnp.int32))
    blk_expert = jnp.where(blk < n_used, blk_expert, last_expert).astype(jnp.int32)
    return dest.reshape(n, TOP_K), src.reshape(n_blocks, 1, DISPATCH_BLOCK), blk_expert


def kernel(x, ln_in_g, ln_in_b, w_in, b_forget, w_up_fox, w_up_sb, w_out, ln1_g, ln1_b, w_group, b_group,
           w_expert_router, b_expert_router, w1, w3, w2, ln2_g, ln2_b):
    bsz, seq, d = x.shape
    n = bsz * seq
    assert w_in.shape[0] == DEPTH
    x2 = x.reshape(n, d)
    scale = HEAD_DIM ** -0.5

    wi = w_in[0]
    c = D_BRANCH
    off_f = 3 * c
    off_s = off_f + N_HEADS
    off_g = off_s + 3 * c
    perm = jnp.argsort(b_forget[0])
    cols = (perm[:, None] * HEAD_DIM + jnp.arange(HEAD_DIM)[None, :]).reshape(c)
    w_main = jnp.concatenate([
        wi[:, 0:c][:, cols] * (scale * LOG2E), wi[:, c:2 * c][:, cols], wi[:, 2 * c:3 * c][:, cols],
        wi[:, off_s:off_s + c] * (scale * LOG2E), wi[:, off_s + c:off_g],
        wi[:, off_g:],
    ], axis=1).astype(_bf16)
    wf = _pad_lanes(wi[:, off_f:off_s][:, perm])
    wf_hi, wf_lo = _split2(wf)
    bf_pad = _pad_lanes(b_forget[0][perm][None, :])
    w_up_fox_p = w_up_fox[0][cols, :]

    row = lambda v: v.reshape(1, -1)
    qf, kf, vf, qs, ks, vs, gf, gs, fcum = _inproj(
        x2, row(ln_in_g), row(ln_in_b), w_main, wf_hi, wf_lo, bf_pad, seq)

    to3 = lambda t: t.reshape(bsz, seq, D_BRANCH)
    nk = seq // FOX_TILE
    f5 = fcum[:, :N_HEADS].reshape(bsz, nk, FOX_TILE, N_HEADS // 2, 2)
    frow = f5.transpose(0, 3, 1, 4, 2)
    fcol = f5.reshape(bsz, seq, N_HEADS // 2, 2).transpose(0, 2, 1, 3)
    fedge = jnp.concatenate([f5[:, :, 0], f5[:, :, FOX_TILE - 1]], axis=-1)
    fedge = fedge.transpose(0, 2, 3, 1)
    vf_aug = vf.reshape(bsz, seq, 2 * D_BRANCH)
    y_fox = _fox_attn(to3(qf), to3(kf), vf_aug, frow, fcol, fedge, FOX_TILE).reshape(n, D_BRANCH)
    y_sb = _sb_attn(to3(qs), to3(ks), to3(vs), SB_TILE).reshape(n, D_BRANCH)

    wr = _pad_lanes(jnp.concatenate([w_group[0], w_expert_router[0]], axis=1))
    wr_hi, wr_lo = _split2(wr)
    br = _pad_lanes(jnp.concatenate([b_group[0], b_expert_router[0]])[None, :])
    h1, route = _post_attn(
        x2, row(ln_in_g), row(ln_in_b), y_fox, y_sb, gf, gs,
        w_up_fox_p.astype(_bf16), w_up_sb[0].astype(_bf16), w_out[0].astype(_bf16),
        row(ln1_g[0]), row(ln1_b[0]), wr_hi, wr_lo, br)

    expert_id = route[:, 2:4].astype(jnp.int32)
    dest, src3, blk_expert = _dispatch_plan(expert_id)
    y_pad = _experts(blk_expert, src3, h1,
                     w1[0].astype(_bf16), w3[0].astype(_bf16), w2[0].astype(_bf16))

    tm = COMBINE_TILE
    pos3 = dest.reshape(n // tm, tm, TOP_K).transpose(0, 2, 1).reshape(n // tm, 1, TOP_K * tm)
    out = _combine(pos3, h1, route, row(ln2_g[0]), row(ln2_b[0]), y_pad)
    return out.reshape(bsz, seq, d)
```

```python
import functools

import jax
import jax.numpy as jnp
from jax import lax
from jax.experimental import pallas as pl
from jax.experimental.pallas import tpu as pltpu

HEAD_DIM = 64
N_HEADS = 8
D_BRANCH = N_HEADS * HEAD_DIM
N_GROUPS = 4
EXPERTS_PER_GROUP = 8
N_EXPERTS = N_GROUPS * EXPERTS_PER_GROUP
TOP_K = 2
DISPATCH_BLOCK = 256
DEPTH = 1
DN_ALPHA = (2.0 * DEPTH) ** 0.25
LN_EPS = 1e-5

LANES = 128
TOKEN_TILE = 512
FOX_TILE = 512
FOX_SAFE_DOT = 60.0
FOX_DEAD = 150.0
SB_TILE = 256
SB_DEAD = 150.0
LOG2E = 1.4426950408889634
COMBINE_TILE = 256
DMA_UNROLL = 8
VMEM_LIMIT = 56 * 1024 * 1024
NEG_BIG = -1e30

_f32 = jnp.float32
_bf16 = jnp.bfloat16


def _layer_norm(x, g, b):
    mu = jnp.mean(x, axis=-1, keepdims=True)
    xc = x - mu
    var = jnp.mean(xc * xc, axis=-1, keepdims=True)
    return xc * lax.rsqrt(var + LN_EPS) * g + b


def _split2(x):
    hi = x.astype(_bf16)
    lo = (x - hi.astype(_f32)).astype(_bf16)
    return hi, lo


def _split3(x):
    hi = x.astype(_bf16)
    r = x - hi.astype(_f32)
    mid = r.astype(_bf16)
    lo = (r - mid.astype(_f32)).astype(_bf16)
    return hi, mid, lo


def _dot(a, b):
    return jnp.dot(a, b, preferred_element_type=_f32)


def _dot_nt(a, b):
    return lax.dot_general(a, b, (((1,), (1,)), ((), ())), preferred_element_type=_f32)


def _inproj_kernel(x_ref, g_ref, b_ref, w_ref, wf_hi_ref, wf_lo_ref, bf_ref, tri_ref,
                   qf_ref, kf_ref, vf_ref, qs_ref, ks_ref, vs_ref, gf_ref, gs_ref, fcum_ref,
                   carry_ref, *, tiles_per_seq):
    i = pl.program_id(0)
    h = _layer_norm(x_ref[...], g_ref[...], b_ref[...])
    h_hi, h_lo = _split2(h)

    col = 0
    for ref in (qf_ref, kf_ref, None, qs_ref, ks_ref, vs_ref, gf_ref, gs_ref):
        width = D_BRANCH if ref is None else ref.shape[-1]
        res = _dot(h_hi, w_ref[:, col:col + width])
        col += width
        if ref is not None:
            ref[...] = res.astype(ref.dtype)
            continue
        lane = lax.broadcasted_iota(jnp.int32, (res.shape[0], LANES), 1)
        one_at = lambda k: jnp.where(lane == k, 1.0, 0.0)
        for p in range(D_BRANCH // LANES):
            vp = res[:, p * LANES:(p + 1) * LANES]
            first = jnp.where(lane < HEAD_DIM, vp, one_at(HEAD_DIM))
            second = jnp.where(lane < HEAD_DIM, one_at(0), vp)
            vf_ref[:, 2 * p * LANES:(2 * p + 1) * LANES] = first.astype(vf_ref.dtype)
            vf_ref[:, (2 * p + 1) * LANES:(2 * p + 2) * LANES] = second.astype(vf_ref.dtype)

    f_logit = (_dot(h_hi, wf_hi_ref[...]) + _dot(h_lo, wf_hi_ref[...])
               + _dot(h_hi, wf_lo_ref[...])) + bf_ref[...]
    log_f = jnp.minimum(f_logit, 0.0) - jnp.log(1.0 + jnp.exp(-jnp.abs(f_logit)))

    p0, p1, p2 = _split3(log_f)
    tri = tri_ref[...]
    cum = _dot(tri, p0) + _dot(tri, p1) + _dot(tri, p2)

    @pl.when(i % tiles_per_seq == 0)
    def _():
        carry_ref[...] = jnp.zeros_like(carry_ref)

    cum = cum + carry_ref[...]
    fcum_ref[...] = cum * LOG2E
    carry_ref[...] = cum[-1:, :]


def _inproj(x2, ln_g, ln_b, w_main, wf_hi, wf_lo, bf_pad, seq_len):
    n, d = x2.shape
    tm = TOKEN_TILE
    assert n % tm == 0 and seq_len % tm == 0
    tri = (lax.broadcasted_iota(jnp.int32, (tm, tm), 1)
           <= lax.broadcasted_iota(jnp.int32, (tm, tm), 0)).astype(_bf16)
    const = lambda i: (0, 0)
    row = lambda i: (i, 0)
    widths = (D_BRANCH, D_BRANCH, 2 * D_BRANCH) + (D_BRANCH,) * 3 + (d, d)
    out_shape = [jax.ShapeDtypeStruct((n, w), _bf16) for w in widths]
    out_shape.append(jax.ShapeDtypeStruct((n, LANES), _f32))
    out_specs = [pl.BlockSpec((tm, w), row) for w in widths] + [pl.BlockSpec((tm, LANES), row)]
    return pl.pallas_call(
        functools.partial(_inproj_kernel, tiles_per_seq=seq_len // tm),
        grid=(n // tm,),
        in_specs=[
            pl.BlockSpec((tm, d), row),
            pl.BlockSpec((1, d), const),
            pl.BlockSpec((1, d), const),
            pl.BlockSpec(w_main.shape, const),
            pl.BlockSpec(wf_hi.shape, const),
            pl.BlockSpec(wf_lo.shape, const),
            pl.BlockSpec((1, LANES), const),
            pl.BlockSpec((tm, tm), const),
        ],
        out_specs=out_specs,
        out_shape=out_shape,
        scratch_shapes=[pltpu.VMEM((1, LANES), _f32)],
        compiler_params=pltpu.CompilerParams(
            dimension_semantics=("arbitrary",), vmem_limit_bytes=VMEM_LIMIT),
        name="inproj",
    )(x2, ln_g, ln_b, w_main, wf_hi, wf_lo, bf_pad, tri)


def _head_masks(shape):
    lane = lax.broadcasted_iota(jnp.int32, shape, 1)
    return lane < HEAD_DIM


def _fox_kernel(q_ref, k_ref, v_ref, frow_ref, fcol_ref, fedge_ref, o_ref, m0_ref, m1_ref, acc0_ref, acc1_ref,
                kmax_ref, *, tile):
    i = pl.program_id(2)
    nk = pl.num_programs(2)
    heads = range(2)
    q = q_ref[0]
    first_head = _head_masks(q.shape)
    zero_q = jnp.zeros_like(q)
    qh = (jnp.where(first_head, q, zero_q), jnp.where(first_head, zero_q, q))
    fcol = [fcol_ref[0, 0, :, h:h + 1] for h in heads]
    row = lax.broadcasted_iota(jnp.int32, (tile, tile), 0)
    colm = lax.broadcasted_iota(jnp.int32, (tile, tile), 1)
    causal = colm <= row
    m_refs = (m0_ref, m1_ref)
    acc_refs = (acc0_ref, acc1_ref)
    in_first = lax.broadcasted_iota(jnp.int32, (LANES, LANES), 0) < HEAD_DIM
    head_sum = (in_first.astype(_bf16), jnp.logical_not(in_first).astype(_bf16))

    def max_sq_norms(x):
        xf = x.astype(_f32)
        sq = (xf * xf).astype(_bf16)
        return [jnp.max(_dot(sq, head_sum[h]), axis=0, keepdims=True) for h in heads]

    def k_tile(j):
        return k_ref[0, pl.ds(pl.multiple_of(j * tile, tile), tile), :]

    def v_tile(j, h):
        return v_ref[0, pl.ds(pl.multiple_of(j * tile, tile), tile), h * LANES:(h + 1) * LANES]

    @pl.when(i == 0)
    def _():
        def body(c, carry):
            n2 = max_sq_norms(k_tile(c))
            return tuple(jnp.maximum(carry[h], n2[h]) for h in heads)
        zero = jnp.zeros((1, LANES), _f32)
        km = lax.fori_loop(0, nk, body, (zero, zero))
        for h in heads:
            kmax_ref[h:h + 1, :] = km[h]

    for h in heads:
        acc_refs[h][...] = jnp.zeros(acc_refs[h].shape, _f32)

    qn2 = max_sq_norms(q)
    bound_sq = [jnp.max(qn2[h] * kmax_ref[h:h + 1, :]) * 1.05 for h in heads]
    safe = jnp.maximum(bound_sq[0], bound_sq[1]) <= FOX_SAFE_DOT * FOX_SAFE_DOT

    def logits(j):
        kt = k_tile(j)
        return [(_dot_nt(qh[h], kt) - frow_ref[0, 0, j, h:h + 1, :]) + fcol[h] for h in heads]

    @pl.when(safe)
    def _():
        def step(j, masked):
            p = [jnp.exp2(t) for t in logits(j)]
            if masked:
                p = [jnp.where(causal, ph, 0.0) for ph in p]
            pv = [_dot(p[h].astype(_bf16), v_tile(j, h)) for h in heads]
            for h in heads:
                acc_refs[h][...] += pv[h]

        def alive(j):
            res = None
            for h in heads:
                x = fedge_ref[0, 0, 2 + h, jnp.maximum(j, 0)] - fedge_ref[0, 0, h, i] - FOX_DEAD
                a = jnp.logical_or(x <= 0.0, bound_sq[h] >= x * x)
                res = a if res is None else jnp.logical_or(res, a)
            return res

        step(i, True)

        def body(j):
            step(j, False)
            return j - 1

        lax.while_loop(lambda j: jnp.logical_and(j >= 0, alive(j)), body, i - 1)

    def run(step):
        def body(j, carry):
            step(j, False)
            return carry
        lax.fori_loop(0, i, body, 0)
        step(i, True)

    @pl.when(jnp.logical_not(safe))
    def _():
        for h in heads:
            m_refs[h][...] = jnp.full(m_refs[h].shape, NEG_BIG, _f32)

        def step(j, masked):
            t = logits(j)
            if masked:
                t = [jnp.where(causal, th, NEG_BIG) for th in t]
            m_old = [m_refs[h][...] for h in heads]
            m_new = [jnp.maximum(m_old[h], jnp.max(t[h], axis=-1, keepdims=True)) for h in heads]
            p = [jnp.exp2(t[h] - m_new[h]).astype(_bf16) for h in heads]
            pv = [_dot(p[h], v_tile(j, h)) for h in heads]
            for h in heads:
                acc_refs[h][...] = jnp.exp2(m_old[h] - m_new[h]) * acc_refs[h][...] + pv[h]
                m_refs[h][...] = m_new[h]
        run(step)

    acc0 = acc0_ref[...]
    acc1 = acc1_ref[...]
    res0 = acc0 / acc0[:, HEAD_DIM:HEAD_DIM + 1]
    res1 = acc1 / acc1[:, 0:1]
    o_ref[0] = jnp.where(_head_masks(res0.shape), res0, res1).astype(o_ref.dtype)


def _fox_attn(q, k, v_aug, frow, fcol, fedge, tile):
    b, s, _ = q.shape
    n_pairs = D_BRANCH // LANES
    nk = s // tile
    return pl.pallas_call(
        functools.partial(_fox_kernel, tile=tile),
        grid=(b, n_pairs, nk),
        in_specs=[
            pl.BlockSpec((1, tile, LANES), lambda bi, hp, i: (bi, i, hp)),
            pl.BlockSpec((1, s, LANES), lambda bi, hp, i: (bi, 0, hp)),
            pl.BlockSpec((1, s, 2 * LANES), lambda bi, hp, i: (bi, 0, hp)),
            pl.BlockSpec((1, 1, nk, 2, tile), lambda bi, hp, i: (bi, hp, 0, 0, 0)),
            pl.BlockSpec((1, 1, tile, 2), lambda bi, hp, i: (bi, hp, i, 0)),
            pl.BlockSpec((1, 1, 4, nk), lambda bi, hp, i: (bi, hp, 0, 0), memory_space=pltpu.SMEM),
        ],
        out_specs=pl.BlockSpec((1, tile, LANES), lambda bi, hp, i: (bi, i, hp)),
        out_shape=jax.ShapeDtypeStruct((b, s, D_BRANCH), _bf16),
        scratch_shapes=[pltpu.VMEM((tile, 1), _f32), pltpu.VMEM((tile, 1), _f32),
                        pltpu.VMEM((tile, LANES), _f32), pltpu.VMEM((tile, LANES), _f32),
                        pltpu.VMEM((8, LANES), _f32)],
        compiler_params=pltpu.CompilerParams(
            dimension_semantics=("arbitrary", "arbitrary", "arbitrary"),
            vmem_limit_bytes=VMEM_LIMIT),
        name="fox_attn",
    )(q, k, v_aug, frow, fcol, fedge)


def _sb_kernel(q_ref, k_ref, v_ref, tri_ref, o_ref, *scratch, tile):
    i = pl.program_id(2)
    heads = range(2)
    subs = range(2)
    c_refs = [[scratch[2 * s + h] for h in heads] for s in subs]
    acc_refs = [[scratch[4 + 2 * s + h] for h in heads] for s in subs]
    first_head = _head_masks((tile, LANES))
    qh = []
    for s in subs:
        q = q_ref[0, s * tile:(s + 1) * tile, :]
        zero = jnp.zeros_like(q)
        qh.append((jnp.where(first_head, q, zero), jnp.where(first_head, zero, q)))
    row = lax.broadcasted_iota(jnp.int32, (tile, tile), 0)
    colm = lax.broadcasted_iota(jnp.int32, (tile, tile), 1)
    strict = colm < row
    tri2 = tri_ref[...]
    sign_bit = jnp.uint32(0x80000000)

    def sweep(tiles, c):
        n_t = range(len(tiles))
        kt = [k_ref[0, pl.ds(pl.multiple_of(t[0] * tile, tile), tile), :] for t in tiles]
        vt = [v_ref[0, pl.ds(pl.multiple_of(t[0] * tile, tile), tile), :] for t in tiles]
        vt = [v if t[2] is None else v * t[2] for v, t in zip(vt, tiles)]
        z = [[_dot_nt(qh[tiles[u][3]][h], kt[u]) for h in heads] for u in n_t]
        neg_abs = [[pltpu.bitcast(pltpu.bitcast(z[u][h], jnp.uint32) | sign_bit, _f32) for h in heads]
                   for u in n_t]
        sp = [[jnp.maximum(z[u][h], 0.0) + jnp.log2(1.0 + jnp.exp2(neg_abs[u][h])) for h in heads]
              for u in n_t]
        sp = [[jnp.where(strict, s, 0.0) for s in sp[u]] if tiles[u][1] else sp[u] for u in n_t]
        incl = [[_dot(jnp.concatenate(_split2(sp[u][h]), axis=1), tri2) for h in heads] for u in n_t]
        c = [list(cs) for cs in c]
        pv = [[None, None] for _ in subs]
        for u in n_t:
            sub = tiles[u][3]
            a = [jnp.exp2(z[u][h] - incl[u][h] - c[sub][h]) for h in heads]
            if tiles[u][1]:
                a = [jnp.where(strict, ah, 0.0) for ah in a]
            for h in heads:
                d = _dot(a[h].astype(_bf16), vt[u])
                pv[sub][h] = d if pv[sub][h] is None else pv[sub][h] + d
                c[sub][h] = c[sub][h] + incl[u][h][:, 0:1]
        return pv, c

    has_prev = (i > 0).astype(_bf16)
    zero_c = jnp.zeros(c_refs[0][0].shape, _f32)
    first = 2 * i
    pv, c = sweep([(first, True, None, 0), (first + 1, True, None, 1),
                   (jnp.maximum(first - 1, 0), False, has_prev, 0), (first, False, None, 1)],
                  [[zero_c, zero_c] for _ in subs])
    for sub in subs:
        for h in heads:
            acc_refs[sub][h][...] = pv[sub][h]
            c_refs[sub][h][...] = c[sub][h]

    for sub in subs:
        def c_min(sub=sub):
            return jnp.min(jnp.minimum(c_refs[sub][0][...], c_refs[sub][1][...]))

        def cond(carry):
            j, cm = carry
            return jnp.logical_and(j >= 0, cm < SB_DEAD)

        def body(carry, sub=sub, c_min=c_min):
            j, _ = carry
            one = [[c_refs[s][h][...] for h in heads] for s in subs]
            pv, c = sweep([(j, False, None, sub)], one)
            for h in heads:
                acc_refs[sub][h][...] += pv[sub][h]
                c_refs[sub][h][...] = c[sub][h]
            return j - 1, c_min()

        lax.while_loop(cond, body, (first + sub - 2, c_min()))
        o_ref[0, sub * tile:(sub + 1) * tile, :] = jnp.where(
            first_head, acc_refs[sub][0][...], acc_refs[sub][1][...]).astype(o_ref.dtype)


def _sb_attn(q, k, v, tile):
    b, s, _ = q.shape
    n_pairs = D_BRANCH // LANES
    assert s % (2 * tile) == 0
    nk = s // (2 * tile)
    tri = (lax.broadcasted_iota(jnp.int32, (tile, tile), 0)
           >= lax.broadcasted_iota(jnp.int32, (tile, tile), 1)).astype(_bf16)
    tri = jnp.concatenate([tri, tri], axis=0)
    return pl.pallas_call(
        functools.partial(_sb_kernel, tile=tile),
        grid=(b, n_pairs, nk),
        in_specs=[
            pl.BlockSpec((1, 2 * tile, LANES), lambda bi, hp, i: (bi, i, hp)),
            pl.BlockSpec((1, s, LANES), lambda bi, hp, i: (bi, 0, hp)),
            pl.BlockSpec((1, s, LANES), lambda bi, hp, i: (bi, 0, hp)),
            pl.BlockSpec((2 * tile, tile), lambda bi, hp, i: (0, 0)),
        ],
        out_specs=pl.BlockSpec((1, 2 * tile, LANES), lambda bi, hp, i: (bi, i, hp)),
        out_shape=jax.ShapeDtypeStruct((b, s, D_BRANCH), _bf16),
        scratch_shapes=[pltpu.VMEM((tile, 1), _f32)] * 4 + [pltpu.VMEM((tile, LANES), _f32)] * 4,
        compiler_params=pltpu.CompilerParams(
            dimension_semantics=("arbitrary", "arbitrary", "arbitrary"),
            vmem_limit_bytes=VMEM_LIMIT),
        name="sb_attn",
    )(q, k, v, tri)


def _post_attn_kernel(x_ref, lng_ref, lnb_ref, yf_ref, ys_ref, gf_ref, gs_ref,
                      wuf_ref, wus_ref, wo_ref, g1_ref, b1_ref, wr_hi_ref, wr_lo_ref, br_ref,
                      h1_ref, route_ref):
    h0 = _layer_norm(x_ref[...], lng_ref[...], lnb_ref[...])
    up_f = _dot(yf_ref[...], wuf_ref[...])
    up_s = _dot(ys_ref[...], wus_ref[...])
    sig_f = 1.0 / (1.0 + jnp.exp(-gf_ref[...].astype(_f32)))
    sig_s = 1.0 / (1.0 + jnp.exp(-gs_ref[...].astype(_f32)))
    merged = sig_f * up_f + sig_s * up_s
    mix = _dot(merged.astype(_bf16), wo_ref[...])
    h1 = _layer_norm(DN_ALPHA * h0 + mix, g1_ref[...], b1_ref[...])
    h1_ref[...] = h1

    h_hi, h_lo = _split2(h1)
    logits = (_dot(h_hi, wr_hi_ref[...]) + _dot(h_lo, wr_hi_ref[...])
              + _dot(h_hi, wr_lo_ref[...])) + br_ref[...]
    lane = lax.broadcasted_iota(jnp.int32, logits.shape, 1)
    lane_f = lane.astype(_f32)
    big = float(LANES)

    g_log = jnp.where(lane < N_GROUPS, logits, -jnp.inf)
    g_max = jnp.max(g_log, axis=-1, keepdims=True)
    g_sel = jnp.min(jnp.where(g_log == g_max, lane_f, big), axis=-1, keepdims=True)
    g_gate = 1.0 / jnp.sum(jnp.exp(g_log - g_max), axis=-1, keepdims=True)

    lo_lane = N_GROUPS + g_sel * EXPERTS_PER_GROUP
    in_grp = jnp.logical_and(lane_f >= lo_lane, lane_f < lo_lane + EXPERTS_PER_GROUP)
    e_log = jnp.where(in_grp, logits, -jnp.inf)
    e1 = jnp.max(e_log, axis=-1, keepdims=True)
    i1 = jnp.min(jnp.where(e_log == e1, lane_f, big), axis=-1, keepdims=True)
    e_log2 = jnp.where(lane_f == i1, -jnp.inf, e_log)
    e2 = jnp.max(e_log2, axis=-1, keepdims=True)
    i2 = jnp.min(jnp.where(e_log2 == e2, lane_f, big), axis=-1, keepdims=True)
    d = jnp.exp(e2 - e1)
    w1 = 1.0 / (1.0 + d)
    w2 = d * w1

    route = jnp.where(lane == 0, g_gate * w1, 0.0)
    route = jnp.where(lane == 1, g_gate * w2, route)
    route = jnp.where(lane == 2, i1 - N_GROUPS, route)
    route = jnp.where(lane == 3, i2 - N_GROUPS, route)
    route_ref[...] = route


def _post_attn(x2, lng, lnb, yf, ys, gf, gs, wuf, wus, wo, g1, b1, wr_hi, wr_lo, br):
    n, d = x2.shape
    tm = TOKEN_TILE
    const = lambda i: (0, 0)
    row = lambda i: (i, 0)
    vec = pl.BlockSpec((1, d), const)
    return pl.pallas_call(
        _post_attn_kernel,
        grid=(n // tm,),
        in_specs=[
            pl.BlockSpec((tm, d), row), vec, vec,
            pl.BlockSpec((tm, D_BRANCH), row), pl.BlockSpec((tm, D_BRANCH), row),
            pl.BlockSpec((tm, d), row), pl.BlockSpec((tm, d), row),
            pl.BlockSpec(wuf.shape, const), pl.BlockSpec(wus.shape, const),
            pl.BlockSpec(wo.shape, const), vec, vec,
            pl.BlockSpec(wr_hi.shape, const), pl.BlockSpec(wr_lo.shape, const),
            pl.BlockSpec((1, LANES), const),
        ],
        out_specs=[pl.BlockSpec((tm, d), row), pl.BlockSpec((tm, LANES), row)],
        out_shape=[jax.ShapeDtypeStruct((n, d), _f32), jax.ShapeDtypeStruct((n, LANES), _f32)],
        compiler_params=pltpu.CompilerParams(
            dimension_semantics=("arbitrary",), vmem_limit_bytes=VMEM_LIMIT),
        name="post_attn",
    )(x2, lng, lnb, yf, ys, gf, gs, wuf, wus, wo, g1, b1, wr_hi, wr_lo, br)


def _experts_kernel(blk_expert_ref, src_ref, src_next_ref, h_hbm, w1_ref, w3_ref, w2_ref,
                    y_ref, buf_ref, sem_ref):
    del blk_expert_ref
    b = pl.program_id(0)
    nb = pl.num_programs(0)
    rows = buf_ref.shape[1]
    slot = b % 2

    def row_copy(idx_ref, r, slot):
        return pltpu.make_async_copy(h_hbm.at[pl.ds(idx_ref[0, 0, r], 1), :],
                                     buf_ref.at[slot, pl.ds(r, 1), :], sem_ref.at[slot])

    @pl.when(b == 0)
    def _():
        def body(r, carry):
            for u in range(DMA_UNROLL):
                row_copy(src_ref, r * DMA_UNROLL + u, 0).start(priority=u % 2)
            return carry
        lax.fori_loop(0, rows // DMA_UNROLL, body, 0)

    pltpu.make_async_copy(buf_ref.at[slot], buf_ref.at[slot], sem_ref.at[slot]).wait()

    def compute(prefetch):
        if prefetch:
            for r in range(rows):
                row_copy(src_next_ref, r, 1 - slot).start(priority=r % 2)
        xb = buf_ref[slot].astype(_bf16)
        a = _dot(xb, w1_ref[0])
        g = _dot(xb, w3_ref[0])
        hidden = (a * (1.0 / (1.0 + jnp.exp(-a)))) * g
        y_ref[...] = _dot(hidden.astype(_bf16), w2_ref[0])

    pl.when(b + 1 < nb)(functools.partial(compute, True))
    pl.when(b + 1 >= nb)(functools.partial(compute, False))


def _experts(blk_expert, src3, h1, w1b, w3b, w2b):
    n_blocks = blk_expert.shape[0]
    rows = DISPATCH_BLOCK
    d = h1.shape[1]
    de = w1b.shape[2]
    grid_spec = pltpu.PrefetchScalarGridSpec(
        num_scalar_prefetch=1,
        grid=(n_blocks,),
        in_specs=[
            pl.BlockSpec((1, 1, rows), lambda b, be: (b, 0, 0), memory_space=pltpu.SMEM),
            pl.BlockSpec((1, 1, rows), lambda b, be: (jnp.minimum(b + 1, n_blocks - 1), 0, 0),
                         memory_space=pltpu.SMEM),
            pl.BlockSpec(memory_space=pl.ANY),
            pl.BlockSpec((1, d, de), lambda b, be: (be[b], 0, 0)),
            pl.BlockSpec((1, d, de), lambda b, be: (be[b], 0, 0)),
            pl.BlockSpec((1, de, d), lambda b, be: (be[b], 0, 0)),
        ],
        out_specs=pl.BlockSpec((rows, d), lambda b, be: (b, 0)),
        scratch_shapes=[pltpu.VMEM((2, rows, d), _f32), pltpu.SemaphoreType.DMA((2,))],
    )
    return pl.pallas_call(
        _experts_kernel,
        grid_spec=grid_spec,
        out_shape=jax.ShapeDtypeStruct((n_blocks * rows, d), _f32),
        compiler_params=pltpu.CompilerParams(
            dimension_semantics=("arbitrary",), vmem_limit_bytes=VMEM_LIMIT),
        name="experts",
    )(blk_expert, src3, src3, h1, w1b, w3b, w2b)


def _combine_kernel(pos_ref, pos_next_ref, h1_ref, gate_ref, g2_ref, b2_ref, y_hbm, o_ref, buf_ref, sem_ref):
    tm = o_ref.shape[0]
    i = pl.program_id(0)
    n_steps = pl.num_programs(0)
    slot = i % 2

    def start_tile(idx_ref, slot):
        def body(r, carry):
            for u in range(DMA_UNROLL):
                rr = r * DMA_UNROLL + u
                pltpu.make_async_copy(y_hbm.at[pl.ds(idx_ref[0, 0, rr], 1), :],
                                      buf_ref.at[slot, pl.ds(rr, 1), :], sem_ref.at[slot]).start(priority=u % 2)
            return carry
        lax.fori_loop(0, TOP_K * tm // DMA_UNROLL, body, 0)

    @pl.when(i == 0)
    def _():
        start_tile(pos_ref, 0)

    pltpu.make_async_copy(buf_ref.at[slot], buf_ref.at[slot], sem_ref.at[slot]).wait()

    def compute(prefetch):
        if prefetch:
            for rr in range(TOP_K * tm):
                pltpu.make_async_copy(y_hbm.at[pl.ds(pos_next_ref[0, 0, rr], 1), :],
                                      buf_ref.at[1 - slot, pl.ds(rr, 1), :],
                                      sem_ref.at[1 - slot]).start(priority=rr % 2)
        gate = gate_ref[...]
        ffn = gate[:, 0:1] * buf_ref[slot, 0:tm, :] + gate[:, 1:2] * buf_ref[slot, tm:2 * tm, :]
        o_ref[...] = _layer_norm(DN_ALPHA * h1_ref[...] + ffn, g2_ref[...], b2_ref[...])

    pl.when(i + 1 < n_steps)(functools.partial(compute, True))
    pl.when(i + 1 >= n_steps)(functools.partial(compute, False))


def _combine(pos3, h1, route, g2, b2, y_pad):
    n, d = h1.shape
    tm = COMBINE_TILE
    const = lambda i: (0, 0)
    row = lambda i: (i, 0)
    return pl.pallas_call(
        _combine_kernel,
        grid=(n // tm,),
        in_specs=[
            pl.BlockSpec((1, 1, TOP_K * tm), lambda i: (i, 0, 0), memory_space=pltpu.SMEM),
            pl.BlockSpec((1, 1, TOP_K * tm), lambda i: (jnp.minimum(i + 1, n // tm - 1), 0, 0),
                         memory_space=pltpu.SMEM),
            pl.BlockSpec((tm, d), row),
            pl.BlockSpec((tm, LANES), row),
            pl.BlockSpec((1, d), const), pl.BlockSpec((1, d), const),
            pl.BlockSpec(memory_space=pl.ANY),
        ],
        out_specs=pl.BlockSpec((tm, d), row),
        out_shape=jax.ShapeDtypeStruct((n, d), _f32),
        scratch_shapes=[pltpu.VMEM((2, TOP_K * tm, d), _f32), pltpu.SemaphoreType.DMA((2,))],
        compiler_params=pltpu.CompilerParams(
            dimension_semantics=("arbitrary",), vmem_limit_bytes=VMEM_LIMIT),
        name="combine",
    )(pos3, pos3, h1, route, g2, b2, y_pad)


def _pad_lanes(w):
    return jnp.pad(w, ((0, 0), (0, LANES - w.shape[1])))


def _dispatch_plan(expert_id):
    n = expert_id.shape[0]
    m = n * TOP_K
    flat = expert_id.reshape(m)
    onehot = (flat[:, None] == jnp.arange(N_EXPERTS, dtype=jnp.int32)[None, :]).astype(jnp.int32)
    incl = jnp.cumsum(onehot, axis=0)
    counts = incl[-1]
    rank = jnp.sum((incl - onehot) * onehot, axis=1)
    blocks_per = (counts + DISPATCH_BLOCK - 1) // DISPATCH_BLOCK
    blk_end = jnp.cumsum(blocks_per)
    pstarts = (blk_end - blocks_per) * DISPATCH_BLOCK
    dest = pstarts[flat] + rank
    n_blocks = (m + N_EXPERTS * (DISPATCH_BLOCK - 1) + DISPATCH_BLOCK - 1) // DISPATCH_BLOCK
    n_used = blk_end[-1]
    src = jnp.zeros((n_blocks * DISPATCH_BLOCK,), jnp.int32).at[dest].set(
        jnp.arange(m, dtype=jnp.int32) // TOP_K)
    blk = jnp.arange(n_blocks, dtype=jnp.int32)
    blk_expert = jnp.sum((blk[:, None] >= blk_end[None, :]).astype(jnp.int32), axis=1)
    last_expert = jnp.sum((jnp.maximum(n_used - 1, 0) >= blk_end).astype(jnp.int32))
    blk_expert = jnp.where(blk < n_used, blk_expert, last_expert).astype(jnp.int32)
    return dest.reshape(n, TOP_K), src.reshape(n_blocks, 1, DISPATCH_BLOCK), blk_expert


def kernel(x, ln_in_g, ln_in_b, w_in, b_forget, w_up_fox, w_up_sb, w_out, ln1_g, ln1_b, w_group, b_group,
           w_expert_router, b_expert_router, w1, w3, w2, ln2_g, ln2_b):
    bsz, seq, d = x.shape
    n = bsz * seq
    assert w_in.shape[0] == DEPTH
    x2 = x.reshape(n, d)
    scale = HEAD_DIM ** -0.5

    wi = w_in[0]
    c = D_BRANCH
    off_f = 3 * c
    off_s = off_f + N_HEADS
    off_g = off_s + 3 * c
    perm = jnp.argsort(b_forget[0])
    cols = (perm[:, None] * HEAD_DIM + jnp.arange(HEAD_DIM)[None, :]).reshape(c)
    w_main = jnp.concatenate([
        wi[:, 0:c][:, cols] * (scale * LOG2E), wi[:, c:2 * c][:, cols], wi[:, 2 * c:3 * c][:, cols],
        wi[:, off_s:off_s + c] * (scale * LOG2E), wi[:, off_s + c:off_g],
        wi[:, off_g:],
    ], axis=1).astype(_bf16)
    wf = _pad_lanes(wi[:, off_f:off_s][:, perm])
    wf_hi, wf_lo = _split2(wf)
    bf_pad = _pad_lanes(b_forget[0][perm][None, :])
    w_up_fox_p = w_up_fox[0][cols, :]

    row = lambda v: v.reshape(1, -1)
    qf, kf, vf, qs, ks, vs, gf, gs, fcum = _inproj(
        x2, row(ln_in_g), row(ln_in_b), w_main, wf_hi, wf_lo, bf_pad, seq)

    to3 = lambda t: t.reshape(bsz, seq, D_BRANCH)
    nk = seq // FOX_TILE
    f5 = fcum[:, :N_HEADS].reshape(bsz, nk, FOX_TILE, N_HEADS // 2, 2)
    frow = f5.transpose(0, 3, 1, 4, 2)
    fcol = f5.reshape(bsz, seq, N_HEADS // 2, 2).transpose(0, 2, 1, 3)
    fedge = jnp.concatenate([f5[:, :, 0], f5[:, :, FOX_TILE - 1]], axis=-1)
    fedge = fedge.transpose(0, 2, 3, 1)
    vf_aug = vf.reshape(bsz, seq, 2 * D_BRANCH)
    y_fox = _fox_attn(to3(qf), to3(kf), vf_aug, frow, fcol, fedge, FOX_TILE).reshape(n, D_BRANCH)
    y_sb = _sb_attn(to3(qs), to3(ks), to3(vs), SB_TILE).reshape(n, D_BRANCH)

    wr = _pad_lanes(jnp.concatenate([w_group[0], w_expert_router[0]], axis=1))
    wr_hi, wr_lo = _split2(wr)
    br = _pad_lanes(jnp.concatenate([b_group[0], b_expert_router[0]])[None, :])
    h1, route = _post_attn(
        x2, row(ln_in_g), row(ln_in_b), y_fox, y_sb, gf, gs,
        w_up_fox_p.astype(_bf16), w_up_sb[0].astype(_bf16), w_out[0].astype(_bf16),
        row(ln1_g[0]), row(ln1_b[0]), wr_hi, wr_lo, br)

    expert_id = route[:, 2:4].astype(jnp.int32)
    dest, src3, blk_expert = _dispatch_plan(expert_id)
    y_pad = _experts(blk_expert, src3, h1,
                     w1[0].astype(_bf16), w3[0].astype(_bf16), w2[0].astype(_bf16))

    tm = COMBINE_TILE
    pos3 = dest.reshape(n // tm, tm, TOP_K).transpose(0, 2, 1).reshape(n // tm, 1, TOP_K * tm)
    out = _combine(pos3, h1, route, row(ln2_g[0]), row(ln2_b[0]), y_pad)
    return out.reshape(bsz, seq, d)
```
